```python
import functools
import jax
import jax.numpy as jnp
from jax import lax
import numpy as np

D_MODEL = 1024
BATCH = 32
SEQ = 2048
DEPTH = 2
DEC_BATCH = 128
DEC_SEQ = 1
PAST_LEN = 16384
PAGE_SIZE = 128

A_HEADS = 8
A_NOPE = 64
A_ROPE = 32
A_V = 64
A_Q_LORA = 384
A_KV_LORA = 256
ROPE_BASE = 10000.0
ATT_BLOCK = 128
SM_SCALE = (A_NOPE + A_ROPE) ** -0.5
B_HEADS = 4
B_DK = 128
B_DV = 128
B_CHUNK = 16
C_WIDTH = D_MODEL
C_KERNEL = 3
D_FF = ((8 * D_MODEL // 3 + 127) // 128) * 128
RMS_EPS = 1e-6

N_AB_LAYERS = (DEPTH + 1) // 2
N_C_LAYERS = DEPTH // 2
AB_SPLITS = (A_Q_LORA, A_KV_LORA, A_ROPE, B_HEADS * B_DK, B_HEADS * B_DK, B_HEADS * B_DV, B_HEADS * B_DV)
AB_IN = A_Q_LORA + A_KV_LORA + A_ROPE + 2 * B_HEADS * B_DK + 2 * B_HEADS * B_DV
AB_MIX = A_HEADS * A_V + B_HEADS * B_DV

kernel_name = 'hybrid_mla_hgrn2_shortconv_macaron_step'


def rms_norm(x, g):
    xf = x.astype(jnp.float32)
    y = xf * lax.rsqrt(jnp.mean(xf * xf, axis=-1, keepdims=True) + RMS_EPS)
    return (y * g.astype(jnp.float32)).astype(x.dtype)


def half_ffn(x, g, w_gate, w_up, w_down):
    h = rms_norm(x, g)
    return x + 0.5 * ((jax.nn.silu(h @ w_gate) * (h @ w_up)) @ w_down)


def rope_tables(pos):
    inv = ROPE_BASE ** (-jnp.arange(0, A_ROPE, 2, dtype=jnp.float32) / A_ROPE)
    ang = pos.astype(jnp.float32)[:, None] * inv[None, :]
    return jnp.cos(ang), jnp.sin(ang)


def apply_rope(x, cos, sin):
    cos = cos.astype(x.dtype)
    sin = sin.astype(x.dtype)
    half = A_ROPE // 2
    x1, x2 = x[..., :half], x[..., half:]
    return jnp.concatenate([x1 * cos - x2 * sin, x1 * sin + x2 * cos], axis=-1)


def mla_prompt(q_nope, q_pe, ckv, kpe, w_ukv):
    b, s = ckv.shape[:2]
    kv = (ckv @ w_ukv).reshape(b, s, A_HEADS, A_NOPE + A_V)
    k_nope, v = kv[..., :A_NOPE], kv[..., A_NOPE:]
    nb = s // ATT_BLOCK
    qn = q_nope.reshape(b, nb, ATT_BLOCK, A_HEADS, A_NOPE).swapaxes(0, 1)
    qp = q_pe.reshape(b, nb, ATT_BLOCK, A_HEADS, A_ROPE).swapaxes(0, 1)
    key_pos = jnp.arange(s)

    def block(args):
        qn_b, qp_b, i = args
        sc = (jnp.einsum('bqhn,bkhn->bhqk', qn_b, k_nope)
              + jnp.einsum('bqhr,bkr->bhqk', qp_b, kpe)).astype(jnp.float32) * SM_SCALE
        q_pos = i * ATT_BLOCK + jnp.arange(ATT_BLOCK)
        sc = jnp.where(key_pos[None, :] <= q_pos[:, None], sc, -jnp.inf)
        p = jax.nn.softmax(sc, axis=-1).astype(v.dtype)
        return jnp.einsum('bhqk,bkhv->bqhv', p, v)

    o = lax.map(block, (qn, qp, jnp.arange(nb)))
    return o.swapaxes(0, 1).reshape(b, s, A_HEADS * A_V)


def mla_sample(q_nope, q_pe, ckv, kpe, pool_ckv, pool_kpe, layer_idx, page_table, w_ukv):
    db, ds = ckv.shape[:2]
    w = w_ukv.reshape(A_KV_LORA, A_HEADS, A_NOPE + A_V)
    w_uk, w_uv = w[..., :A_NOPE], w[..., A_NOPE:]
    q_lat = jnp.einsum('bshn,chn->bshc', q_nope, w_uk)
    ckv_past = pool_ckv[layer_idx, page_table].reshape(db, -1, A_KV_LORA)
    kpe_past = pool_kpe[layer_idx, page_table].reshape(db, -1, A_ROPE)
    n_past = ckv_past.shape[1]
    sc_past = (jnp.einsum('bshc,btc->bhst', q_lat, ckv_past)
               + jnp.einsum('bshr,btr->bhst', q_pe, kpe_past)).astype(jnp.float32) * SM_SCALE
    sc_self = (jnp.einsum('bshc,btc->bhst', q_lat, ckv)
               + jnp.einsum('bshr,btr->bhst', q_pe, kpe)).astype(jnp.float32) * SM_SCALE
    causal = jnp.tril(jnp.ones((ds, ds), dtype=bool))
    sc_self = jnp.where(causal, sc_self, -jnp.inf)
    p = jax.nn.softmax(jnp.concatenate([sc_past, sc_self], axis=-1), axis=-1).astype(ckv.dtype)
    o_lat = (jnp.einsum('bhst,btc->bshc', p[..., :n_past], ckv_past)
             + jnp.einsum('bhst,btc->bshc', p[..., n_past:], ckv))
    o = jnp.einsum('bshc,chv->bshv', o_lat, w_uv)
    return o.reshape(db, ds, A_HEADS * A_V)


def hgrn2(q_raw, f_raw, i_raw, g_raw, lb, g_out, s0):
    b, L = q_raw.shape[:2]
    f32 = jnp.float32
    q = jax.nn.silu(q_raw.astype(f32)) * B_DK ** -0.5
    f = lb + (1.0 - lb) * jax.nn.sigmoid(f_raw.astype(f32))
    k = 1.0 - f
    logf = jnp.log(f)
    v = i_raw.astype(f32)
    pad = (-L) % B_CHUNK
    n = (L + pad) // B_CHUNK

    def blocks(t, d):
        t = jnp.pad(t, ((0, 0), (0, pad), (0, 0)))
        return t.reshape(b, n, B_CHUNK, B_HEADS, d)

    q, k, logf, v = blocks(q, B_DK), blocks(k, B_DK), blocks(logf, B_DK), blocks(v, B_DV)
    bcum = jnp.cumsum(logf, axis=2)
    qe = q * jnp.exp(bcum)
    ke = k * jnp.exp(-bcum)
    kd = k * jnp.exp(bcum[:, :, -1:] - bcum)
    decay = jnp.exp(bcum[:, :, -1])
    causal = jnp.tril(jnp.ones((B_CHUNK, B_CHUNK), dtype=bool))
    att = jnp.where(causal, jnp.einsum('bntha,bnsha->bnhts', qe, ke), 0.0)
    o_intra = jnp.einsum('bnhts,bnshv->bnthv', att, v)

    def step(S, xs):
        qe_c, kd_c, v_c, dec_c = xs
        o_c = jnp.einsum('btha,bhav->bthv', qe_c, S)
        S = dec_c[..., None] * S + jnp.einsum('btha,bthv->bhav', kd_c, v_c)
        return S, o_c

    s_fin, o_inter = lax.scan(step, s0, (qe.swapaxes(0, 1), kd.swapaxes(0, 1), v.swapaxes(0, 1), decay.swapaxes(0, 1)))
    o = (o_intra + o_inter.swapaxes(0, 1)).reshape(b, n * B_CHUNK, B_HEADS, B_DV)[:, :L]
    o = o * lax.rsqrt(jnp.mean(o * o, axis=-1, keepdims=True) + RMS_EPS) * g_out.astype(f32)
    gate = jax.nn.silu(g_raw.astype(f32)).reshape(b, L, B_HEADS, B_DV)
    return (o * gate).reshape(b, L, B_HEADS * B_DV).astype(q_raw.dtype), s_fin


def ab_mixer(h, cos, sin, s0, attend, w_in, g_q, g_kv, w_uq, lb, g_out, w_out):
    b, L = h.shape[:2]
    cuts = [int(c) for c in np.cumsum(AB_SPLITS)[:-1]]
    c_q, c_kv, k_pe, q_b, f_b, i_b, g_b = jnp.split(h @ w_in, cuts, axis=-1)
    cq = rms_norm(c_q, g_q)
    q = (cq @ w_uq).reshape(b, L, A_HEADS, A_NOPE + A_ROPE)
    q_nope = q[..., :A_NOPE]
    q_pe = apply_rope(q[..., A_NOPE:], cos[:, None], sin[:, None])
    ckv = rms_norm(c_kv, g_kv)
    kpe = apply_rope(k_pe, cos, sin)
    o_a = attend(q_nope, q_pe, ckv, kpe)
    o_b, s_new = hgrn2(q_b, f_b, i_b, g_b, lb, g_out, s0)
    y = jnp.concatenate([o_a, o_b], axis=-1) @ w_out
    return y, ckv, kpe, s_new


def c_mixer(h, buf, w_in, w_conv, w_out):
    bg, cg, u = jnp.split(h @ w_in, 3, axis=-1)
    v = cg * u
    L = v.shape[1]
    ext = jnp.concatenate([buf.astype(v.dtype), v], axis=1)
    y = w_conv[0] * ext[:, 0:L]
    for tap in range(1, C_KERNEL):
        y = y + w_conv[tap] * ext[:, tap:tap + L]
    return (bg * y) @ w_out, ext[:, L:]


def setup_inputs(seed: int = 0) -> dict:
    key = jax.random.key(seed)
    ks = jax.random.split(key, 24)
    f32 = jnp.float32
    n_pages = PAST_LEN // PAGE_SIZE
    n_pool = (DEC_BATCH * n_pages * 5) // 4

    def w(k, shape, fan_in):
        return jax.random.normal(k, shape, f32) * (fan_in ** -0.5)

    def gain(k, shape):
        return 1.0 + 0.05 * jax.random.normal(k, shape, f32)

    page_table = jax.random.permutation(ks[6], n_pool)[:DEC_BATCH * n_pages].reshape(DEC_BATCH, n_pages).astype(jnp.int32)
    return {
        'x_prompt': jax.random.normal(ks[0], (BATCH, SEQ, D_MODEL), f32),
        'x_sample': jax.random.normal(ks[1], (DEC_BATCH, DEC_SEQ, D_MODEL), f32),
        'cache_ckv': jax.random.normal(ks[2], (N_AB_LAYERS, n_pool, PAGE_SIZE, A_KV_LORA), f32),
        'cache_kpe': jax.random.normal(ks[3], (N_AB_LAYERS, n_pool, PAGE_SIZE, A_ROPE), f32),
        'state_hgrn': 0.5 * jax.random.normal(ks[4], (N_AB_LAYERS, DEC_BATCH, B_HEADS, B_DK, B_DV), f32),
        'state_conv': jax.random.normal(ks[5], (N_C_LAYERS, DEC_BATCH, C_KERNEL - 1, C_WIDTH), f32),
        'page_table': page_table,
        'norm_g': gain(ks[7], (DEPTH, 3, D_MODEL)),
        'final_norm_g': gain(ks[8], (D_MODEL,)),
        'w_ffn_gate': w(ks[9], (DEPTH, 2, D_MODEL, D_FF), D_MODEL),
        'w_ffn_up': w(ks[10], (DEPTH, 2, D_MODEL, D_FF), D_MODEL),
        'w_ffn_down': w(ks[11], (DEPTH, 2, D_FF, D_MODEL), D_FF),
        'w_in_ab': w(ks[12], (N_AB_LAYERS, D_MODEL, AB_IN), D_MODEL),
        'g_q_lora': gain(ks[13], (N_AB_LAYERS, A_Q_LORA)),
        'g_kv_lora': gain(ks[14], (N_AB_LAYERS, A_KV_LORA)),
        'w_uq': w(ks[15], (N_AB_LAYERS, A_Q_LORA, A_HEADS * (A_NOPE + A_ROPE)), A_Q_LORA),
        'w_ukv': w(ks[16], (N_AB_LAYERS, A_KV_LORA, A_HEADS * (A_NOPE + A_V)), A_KV_LORA),
        'hgrn_lower_bound': 0.1 * jax.random.normal(ks[17], (DEPTH + 1, B_HEADS * B_DK), f32),
        'g_hgrn_out': gain(ks[18], (N_AB_LAYERS, B_DV)),
        'w_out_ab': w(ks[19], (N_AB_LAYERS, AB_MIX, D_MODEL), AB_MIX),
        'w_in_c': w(ks[20], (N_C_LAYERS, D_MODEL, 3 * C_WIDTH), D_MODEL),
        'w_conv_c': w(ks[21], (N_C_LAYERS, C_KERNEL, C_WIDTH), C_KERNEL),
        'w_out_c': w(ks[22], (N_C_LAYERS, C_WIDTH, D_MODEL), C_WIDTH),
    }


def reference(x_prompt, x_sample, cache_ckv, cache_kpe, state_hgrn, state_conv, page_table,
              norm_g, final_norm_g, w_ffn_gate, w_ffn_up, w_ffn_down,
              w_in_ab, g_q_lora, g_kv_lora, w_uq, w_ukv, hgrn_lower_bound, g_hgrn_out, w_out_ab,
              w_in_c, w_conv_c, w_out_c):
    b_p, s_p = x_prompt.shape[:2]
    s_d = x_sample.shape[1]
    past_len = page_table.shape[1] * cache_ckv.shape[2]
    cos_p, sin_p = rope_tables(jnp.arange(s_p))
    cos_d, sin_d = rope_tables(past_len + jnp.arange(s_d))
    lb_all = jnp.cumsum(jax.nn.softmax(hgrn_lower_bound.astype(jnp.float32), axis=0), axis=0)

    xp, xd = x_prompt, x_sample
    ckv_p_l, kpe_p_l, ckv_d_l, kpe_d_l = [], [], [], []
    hg_p_l, hg_d_l, cv_p_l, cv_d_l = [], [], [], []
    for layer in range(DEPTH):
        ffn_a = (w_ffn_gate[layer, 0], w_ffn_up[layer, 0], w_ffn_down[layer, 0])
        ffn_b = (w_ffn_gate[layer, 1], w_ffn_up[layer, 1], w_ffn_down[layer, 1])
        xp = half_ffn(xp, norm_g[layer, 0], *ffn_a)
        xd = half_ffn(xd, norm_g[layer, 0], *ffn_a)
        hp = rms_norm(xp, norm_g[layer, 1])
        hd = rms_norm(xd, norm_g[layer, 1])
        if layer % 2 == 0:
            j = layer // 2
            shared = (w_in_ab[j], g_q_lora[j], g_kv_lora[j], w_uq[j], lb_all[layer], g_hgrn_out[j], w_out_ab[j])
            s0_p = jnp.zeros((b_p, B_HEADS, B_DK, B_DV), jnp.float32)
            attend_p = functools.partial(mla_prompt, w_ukv=w_ukv[j])
            attend_d = functools.partial(mla_sample, pool_ckv=cache_ckv, pool_kpe=cache_kpe,
                                         layer_idx=j, page_table=page_table, w_ukv=w_ukv[j])
            yp, ckv_p, kpe_p, sp = ab_mixer(hp, cos_p, sin_p, s0_p, attend_p, *shared)
            yd, ckv_d, kpe_d, sd = ab_mixer(hd, cos_d, sin_d, state_hgrn[j].astype(jnp.float32), attend_d, *shared)
            ckv_p_l.append(ckv_p)
            kpe_p_l.append(kpe_p)
            ckv_d_l.append(ckv_d)
            kpe_d_l.append(kpe_d)
            hg_p_l.append(sp.astype(x_prompt.dtype))
            hg_d_l.append(sd.astype(x_sample.dtype))
        else:
            j = layer // 2
            buf_p = jnp.zeros((b_p, C_KERNEL - 1, C_WIDTH), xp.dtype)
            yp, bp = c_mixer(hp, buf_p, w_in_c[j], w_conv_c[j], w_out_c[j])
            yd, bd = c_mixer(hd, state_conv[j], w_in_c[j], w_conv_c[j], w_out_c[j])
            cv_p_l.append(bp)
            cv_d_l.append(bd)
        xp = xp + yp
        xd = xd + yd
        xp = half_ffn(xp, norm_g[layer, 2], *ffn_b)
        xd = half_ffn(xd, norm_g[layer, 2], *ffn_b)

    y_prompt = rms_norm(xp, final_norm_g)
    y_sample = rms_norm(xd, final_norm_g)
    return (y_prompt, y_sample,
            jnp.stack(ckv_p_l), jnp.stack(kpe_p_l), jnp.stack(ckv_d_l), jnp.stack(kpe_d_l),
            jnp.stack(hg_p_l), jnp.stack(hg_d_l), jnp.stack(cv_p_l), jnp.stack(cv_d_l))
```

```python
import functools

import jax
import jax.numpy as jnp
from jax import lax
from jax.experimental import pallas as pl
from jax.experimental.pallas import tpu as pltpu

F32 = jnp.float32
BF16 = jnp.bfloat16

A_HEADS = 8
A_NOPE = 64
A_ROPE = 32
A_V = 64
A_Q_LORA = 384
A_KV_LORA = 256
ROPE_BASE = 10000.0
SM_SCALE = (A_NOPE + A_ROPE) ** -0.5
B_HEADS = 4
B_DK = 128
B_DV = 128
B_CHUNK = 16
C_KERNEL = 3
RMS_EPS = 1e-6

LANES = 128
HEAD_SLOT = LANES
HGRN_BLOCK = 128
VMEM_LIMIT = 56 * 1024 * 1024


def _params(sem, vmem=VMEM_LIMIT):
    return pltpu.CompilerParams(dimension_semantics=sem, vmem_limit_bytes=vmem)


def _const_spec(shape):
    nd = len(shape)
    return pl.BlockSpec(shape, lambda *_: (0,) * nd, pipeline_mode=pl.Buffered(1))


def _rms(x, g):
    return x * lax.rsqrt(jnp.mean(x * x, axis=-1, keepdims=True) + RMS_EPS) * g


def _dot(a, b):
    return jnp.dot(a, b, preferred_element_type=F32)


def _dot_nt(a, b):
    return lax.dot_general(a, b, (((1,), (1,)), ((), ())), preferred_element_type=F32)


def _dot_tn(a, b):
    return lax.dot_general(a, b, (((0,), (0,)), ((), ())), preferred_element_type=F32)


def _silu(x):
    return x * jax.nn.sigmoid(x)


def _row_tile(n, want):
    t = min(n, want)
    assert n % t == 0, (n, t)
    return t


def _ffn_body(*refs, n_chunks, tf, n_pre, has_final):
    x_ref = refs[0]
    pre = refs[1:1 + 2 * n_pre]
    g_ref, wg_ref, wu_ref, wd_ref = refs[1 + 2 * n_pre:5 + 2 * n_pre]
    rest = refs[5 + 2 * n_pre:]
    gf_ref = rest[0] if has_final else None
    o_ref = rest[-1]

    x = x_ref[...]
    for p in range(n_pre):
        x = x + _dot(pre[2 * p][...], pre[2 * p + 1][...])
    h = _rms(x, g_ref[...]).astype(BF16)
    acc = None
    for c in range(n_chunks):
        cs = slice(c * tf, (c + 1) * tf)
        gate = _dot(h, wg_ref[:, cs])
        up = _dot(h, wu_ref[:, cs])
        a = (_silu(gate) * up).astype(BF16)
        part = _dot(a, wd_ref[cs, :])
        acc = part if acc is None else acc + part
    y = x + 0.5 * acc
    if has_final:
        y = _rms(y, gf_ref[...])
    o_ref[...] = y


def _ffn(x, g, wg, wu, wd, pre=(), final_g=None, tm=512):
    n, d = x.shape
    f = wg.shape[1]
    tf = 256 if f % 256 == 0 else LANES
    tm = _row_tile(n, tm)
    row = lambda i: (i, 0)
    in_specs = [pl.BlockSpec((tm, d), row)]
    args = [x]
    for act, w in pre:
        in_specs += [pl.BlockSpec((tm, act.shape[1]), row), _const_spec(w.shape)]
        args += [act, w]
    in_specs += [_const_spec((1, d)), _const_spec(wg.shape), _const_spec(wu.shape), _const_spec(wd.shape)]
    args += [g.reshape(1, d), wg, wu, wd]
    if final_g is not None:
        in_specs.append(_const_spec((1, d)))
        args.append(final_g.reshape(1, d))
    body = functools.partial(_ffn_body, n_chunks=f // tf, tf=tf, n_pre=len(pre), has_final=final_g is not None)
    return pl.pallas_call(
        body,
        grid=(n // tm,),
        in_specs=in_specs,
        out_specs=pl.BlockSpec((tm, d), row),
        out_shape=jax.ShapeDtypeStruct((n, d), F32),
        compiler_params=_params(("parallel",)),
        name="half_ffn",
    )(*args)


def _ab_pre_body(x_ref, g_ref, win_ref, gq_ref, gkv_ref, wq_ref, wkv_ref, cos_ref, sin_ref, lb_ref,
                 ckv_ref, kpe_ref, q_ref, k_ref, v_ref, hq_ref, hf_ref, hi_ref, hg_ref, *, layer):
    hb = B_HEADS * B_DK
    h = _rms(x_ref[...], g_ref[...]).astype(BF16)
    proj = _dot(h, win_ref[...])
    o_kv = A_Q_LORA
    o_pa = o_kv + A_KV_LORA
    o_pb = o_pa + HEAD_SLOT
    o_h = o_pb + HEAD_SLOT
    cos = cos_ref[...]
    sin = sin_ref[...]

    cq = _rms(proj[:, :A_Q_LORA], gq_ref[...]).astype(BF16)
    qq = _dot(cq, wq_ref[...])
    hw = A_HEADS * HEAD_SLOT
    for hd in range(A_HEADS):
        sl = slice(hd * HEAD_SLOT, (hd + 1) * HEAD_SLOT)
        sl2 = slice(hw + hd * HEAD_SLOT, hw + (hd + 1) * HEAD_SLOT)
        q_ref[:, sl] = (qq[:, sl] * cos + qq[:, sl2] * sin).astype(BF16)

    ckv = _rms(proj[:, o_kv:o_pa], gkv_ref[...])
    ckv_ref[...] = ckv
    kpe = proj[:, o_pa:o_pb] * cos + proj[:, o_pb:o_h] * sin
    kpe_ref[...] = kpe
    kv = _dot(ckv.astype(BF16), wkv_ref[...])
    for hd in range(A_HEADS):
        sl = slice(hd * HEAD_SLOT, (hd + 1) * HEAD_SLOT)
        k_ref[:, sl] = (kv[:, sl] + kpe).astype(BF16)
    v_ref[...] = kv[:, hw:].astype(BF16)

    lbr = lb_ref[...]
    e = jnp.exp(lbr - jnp.max(lbr, axis=0, keepdims=True))
    lb = jnp.sum(e[:layer + 1], axis=0, keepdims=True) / jnp.sum(e, axis=0, keepdims=True)
    hq_ref[...] = _silu(proj[:, o_h:o_h + hb]) * (B_DK ** -0.5)
    hf_ref[...] = lb + (1.0 - lb) * jax.nn.sigmoid(proj[:, o_h + hb:o_h + 2 * hb])
    hi_ref[...] = proj[:, o_h + 2 * hb:o_h + 3 * hb].astype(BF16)
    hg_ref[...] = _silu(proj[:, o_h + 3 * hb:o_h + 4 * hb])


def _ab_pre(x, g, w, cos, sin, lb_raw, layer, tm=512):
    n, d = x.shape
    tm = _row_tile(cos.shape[0], tm)
    assert n % tm == 0
    n_pos_tiles = cos.shape[0] // tm
    row = lambda i: (i, 0)
    pos = lambda i: (i % n_pos_tiles, 0)
    hw = A_HEADS * HEAD_SLOT
    hb = B_HEADS * B_DK
    outs = [
        (A_KV_LORA, F32), (HEAD_SLOT, F32), (hw, BF16), (hw, BF16), (A_HEADS * A_V, BF16),
        (hb, F32), (hb, F32), (hb, BF16), (hb, F32),
    ]
    return pl.pallas_call(
        functools.partial(_ab_pre_body, layer=layer),
        grid=(n // tm,),
        in_specs=[
            pl.BlockSpec((tm, d), row), _const_spec((1, d)), _const_spec(w["w_in"].shape),
            _const_spec((1, A_Q_LORA)), _const_spec((1, A_KV_LORA)), _const_spec(w["w_q"].shape),
            _const_spec(w["w_kv"].shape), pl.BlockSpec((tm, HEAD_SLOT), pos), pl.BlockSpec((tm, HEAD_SLOT), pos),
            _const_spec(lb_raw.shape),
        ],
        out_specs=[pl.BlockSpec((tm, c), row) for c, _ in outs],
        out_shape=[jax.ShapeDtypeStruct((n, c), dt) for c, dt in outs],
        compiler_params=_params(("parallel",)),
        name="ab_input_stage",
    )(x, g.reshape(1, d), w["w_in"], w["g_q"], w["g_kv"], w["w_q"], w["w_kv"], cos, sin, lb_raw)


def _attn_prompt_body(q_ref, k_ref, v_ref, o_ref, *, tq):
    s_len = q_ref.shape[1]
    nq = s_len // tq
    lane = lax.broadcasted_iota(jnp.int32, (tq, 2 * A_V), 1)
    r_i = lax.broadcasted_iota(jnp.int32, (tq, tq), 0)
    c_i = lax.broadcasted_iota(jnp.int32, (tq, tq), 1)
    causal = c_i <= r_i
    for qi in range(nq):
        rows = slice(qi * tq, (qi + 1) * tq)
        n_past = qi * tq
        outs = []
        for hh in range(2):
            hs = slice(hh * HEAD_SLOT, (hh + 1) * HEAD_SLOT)
            q = q_ref[0, rows, hs]
            s_d = jnp.where(causal, _dot_nt(q, k_ref[0, rows, hs]) * SM_SCALE, -jnp.inf)
            m = jnp.max(s_d, axis=-1, keepdims=True)
            if n_past:
                s_p = _dot_nt(q, k_ref[0, :n_past, hs]) * SM_SCALE
                m = jnp.maximum(m, jnp.max(s_p, axis=-1, keepdims=True))
            p_d = jnp.exp(s_d - m)
            l = jnp.sum(p_d, axis=-1, keepdims=True)
            o = _dot(p_d.astype(BF16), v_ref[0, rows, :])
            if n_past:
                p_p = jnp.exp(s_p - m)
                l = l + jnp.sum(p_p, axis=-1, keepdims=True)
                o = o + _dot(p_p.astype(BF16), v_ref[0, :n_past, :])
            outs.append(o / l)
        o_ref[0, rows, :] = jnp.where(lane < A_V, outs[0], outs[1]).astype(BF16)


def _attn_prompt(q, k, v, tq=256):
    b, s, _ = q.shape
    tq = _row_tile(s, tq)
    blk2 = lambda i, j: (i, 0, j)
    return pl.pallas_call(
        functools.partial(_attn_prompt_body, tq=tq),
        grid=(b, A_HEADS // 2),
        in_specs=[
            pl.BlockSpec((1, s, 2 * HEAD_SLOT), blk2), pl.BlockSpec((1, s, 2 * HEAD_SLOT), blk2),
            pl.BlockSpec((1, s, 2 * A_V), blk2),
        ],
        out_specs=pl.BlockSpec((1, s, 2 * A_V), blk2),
        out_shape=jax.ShapeDtypeStruct((b, s, A_HEADS * A_V), BF16),
        compiler_params=_params(("parallel", "parallel")),
        name="mla_prompt_attention",
    )(q, k, v)


def _hgrn_prompt_body(q_ref, f_ref, i_ref, gt_ref, go_ref, o_ref, st_ref, st_scr, *, n_sb):
    R = HGRN_BLOCK
    hb = B_HEADS * B_DK
    t = pl.program_id(1)

    @pl.when(t == 0)
    def _():
        st_scr[...] = jnp.zeros_like(st_scr)

    r_i = lax.broadcasted_iota(jnp.int32, (R, R), 0)
    c_i = lax.broadcasted_iota(jnp.int32, (R, R), 1)
    tri = jnp.where(c_i <= r_i, 1.0, 0.0).astype(BF16)
    base_mask = ((r_i >> 4) == (c_i >> 4)) & (c_i <= r_i)
    levels = []
    m = B_CHUNK
    while 2 * m <= R:
        sh = (2 * m).bit_length() - 1
        levels.append((m, ((r_i >> sh) == (c_i >> sh)) & ((r_i & (2 * m - 1)) >= m) & ((c_i & (2 * m - 1)) < m)))
        m *= 2
    go = go_ref[...]

    for sb in range(n_sb):
        rows = slice(sb * R, (sb + 1) * R)
        q = q_ref[0, rows, :]
        f = f_ref[0, rows, :]
        v = i_ref[0, rows, :]
        k = 1.0 - f
        lf = jnp.log(f)
        hi = lf.astype(BF16)
        r1 = lf - hi.astype(F32)
        mid = r1.astype(BF16)
        lo = (r1 - mid.astype(F32)).astype(BF16)
        G = _dot(tri, hi) + _dot(tri, mid) + _dot(tri, lo)
        g_last = G[R - 1:R, :]

        def ref_rows(idx, n):
            parts = [jnp.zeros((n, hb), F32) if i < 0 else jnp.broadcast_to(G[i:i + 1, :], (n, hb)) for i in idx]
            return jnp.concatenate(parts, axis=0)

        g0 = ref_rows([B_CHUNK * j - 1 for j in range(R // B_CHUNK)], B_CHUNK)
        q_int = (q * jnp.exp(G)).astype(BF16)
        k_st = (k * jnp.exp(g_last - G)).astype(BF16)
        q_lv = [(q * jnp.exp(G - g0)).astype(BF16)]
        k_lv = [(k * jnp.exp(g0 - G)).astype(BF16)]
        for mm, _ in levels:
            gm = ref_rows([2 * mm * j + mm - 1 for j in range(R // (2 * mm))], 2 * mm)
            q_lv.append((q * jnp.exp(jnp.minimum(G - gm, 0.0))).astype(BF16))
            k_lv.append((k * jnp.exp(jnp.minimum(gm - G, 0.0))).astype(BF16))
        decay = jnp.exp(g_last)

        for hd in range(B_HEADS):
            hs = slice(hd * B_DK, (hd + 1) * B_DK)
            att = jnp.where(base_mask, _dot_nt(q_lv[0][:, hs], k_lv[0][:, hs]), 0.0)
            for li, (_, msk) in enumerate(levels):
                att = jnp.where(msk, _dot_nt(q_lv[li + 1][:, hs], k_lv[li + 1][:, hs]), att)
            st = st_scr[hd]
            o = _dot(att.astype(BF16), v[:, hs]) + _dot_nt(q_int[:, hs], st.astype(BF16))
            st_scr[hd] = st * decay[:, hs] + _dot_tn(v[:, hs], k_st[:, hs])
            o = o * lax.rsqrt(jnp.mean(o * o, axis=-1, keepdims=True) + RMS_EPS) * go
            o_ref[0, rows, hs] = (o * gt_ref[0, rows, hs]).astype(BF16)

    @pl.when(t == pl.num_programs(1) - 1)
    def _():
        for hd in range(B_HEADS):
            st_ref[0, hd] = st_scr[hd].T


def _hgrn_prompt(hq, hf, hi, hg, g_out, rows_per_step=256):
    b, s, hb = hq.shape
    assert s % HGRN_BLOCK == 0
    ts = _row_tile(s, rows_per_step)
    blk = lambda i, t: (i, t, 0)
    return pl.pallas_call(
        functools.partial(_hgrn_prompt_body, n_sb=ts // HGRN_BLOCK),
        grid=(b, s // ts),
        in_specs=[pl.BlockSpec((1, ts, hb), blk)] * 4 + [pl.BlockSpec((1, B_DV), lambda i, t: (0, 0))],
        out_specs=[
            pl.BlockSpec((1, ts, hb), blk),
            pl.BlockSpec((1, B_HEADS, B_DK, B_DV), lambda i, t: (i, 0, 0, 0)),
        ],
        out_shape=[
            jax.ShapeDtypeStruct((b, s, hb), BF16),
            jax.ShapeDtypeStruct((b, B_HEADS, B_DK, B_DV), F32),
        ],
        scratch_shapes=[pltpu.VMEM((B_HEADS, B_DV, B_DK), F32)],
        compiler_params=_params(("parallel", "arbitrary")),
        name="hgrn2_prompt",
    )(hq, hf, hi, hg, g_out.reshape(1, B_DV))


def _hgrn_step_body(q_ref, f_ref, i_ref, gt_ref, go_ref, s_ref, o_ref, so_ref, *, nb):
    go = go_ref[...]
    pad = jnp.zeros((B_DK - 3 * nb, B_DK), F32)
    for hd in range(B_HEADS):
        hs = slice(hd * B_DK, (hd + 1) * B_DK)
        f = f_ref[:, hs]
        cols = jnp.concatenate([q_ref[:, hs], f, 1.0 - f, pad], axis=0).T
        for j in range(nb):
            s_new = cols[:, nb + j:nb + j + 1] * s_ref[j, hd] + cols[:, 2 * nb + j:2 * nb + j + 1] * i_ref[j:j + 1, hs].astype(F32)
            so_ref[j, hd] = s_new
            o = jnp.sum(cols[:, j:j + 1] * s_new, axis=0, keepdims=True)
            o = o * lax.rsqrt(jnp.mean(o * o, axis=-1, keepdims=True) + RMS_EPS) * go
            o_ref[j:j + 1, hs] = (o * gt_ref[j:j + 1, hs]).astype(BF16)


def _hgrn_step(hq, hf, hi, hg, g_out, state, layer, nb=8):
    n, hb = hq.shape
    nb = _row_tile(n, nb)
    row = lambda i: (i, 0)
    st_in = lambda i: (layer, i, 0, 0, 0)
    st_out = lambda i: (i, 0, 0, 0)
    return pl.pallas_call(
        functools.partial(_hgrn_step_body, nb=nb),
        grid=(n // nb,),
        in_specs=[pl.BlockSpec((nb, hb), row)] * 4 + [_const_spec((1, B_DV)), pl.BlockSpec((None, nb, B_HEADS, B_DK, B_DV), st_in)],
        out_specs=[pl.BlockSpec((nb, hb), row), pl.BlockSpec((nb, B_HEADS, B_DK, B_DV), st_out)],
        out_shape=[jax.ShapeDtypeStruct((n, hb), BF16), jax.ShapeDtypeStruct(state.shape[1:], F32)],
        compiler_params=_params(("parallel",)),
        name="hgrn2_step",
    )(hq, hf, hi, hg, g_out.reshape(1, B_DV), state)


def _small_matmul_body(a_ref, w_ref, o_ref):
    o_ref[...] = _dot(a_ref[...].astype(BF16), w_ref[...]).astype(o_ref.dtype)


def _small_matmul(a, w, out_dtype):
    n = a.shape[0]
    out_spec = pl.BlockSpec((n, w.shape[1]), lambda i: (0, 0))
    return pl.pallas_call(
        _small_matmul_body,
        grid=(1,),
        in_specs=[_const_spec(a.shape), _const_spec(w.shape)],
        out_specs=out_spec,
        out_shape=jax.ShapeDtypeStruct((n, w.shape[1]), out_dtype),
        compiler_params=_params(("arbitrary",)),
        name="sample_matmul",
    )(a, w)


def _page_copies(pt_ref, ckv_hbm, kpe_hbm, ckv_buf, kpe_buf, sems, step, slot, *, n_chunk, pc, page):
    b = step // n_chunk
    c = step % n_chunk
    copies = []
    for p in range(pc):
        pg = pt_ref[b, c * pc + p]
        copies.append(pltpu.make_async_copy(ckv_hbm.at[pg], ckv_buf.at[slot, pl.ds(p * page, page), :], sems.at[slot, 0]))
        copies.append(pltpu.make_async_copy(kpe_hbm.at[pg], kpe_buf.at[slot, pl.ds(p * page, page), :], sems.at[slot, 1]))
    return copies


def _attn_decode_body(pt_ref, ql_ref, qp_ref, cn_ref, kn_ref, ckv_hbm, kpe_hbm, o_ref,
                      ckv_buf, kpe_buf, sems, m_scr, l_scr, acc_scr, *, layer, n_chunk, pc, page):
    s = pl.program_id(0)
    total = pl.num_programs(0)
    slot = s % 2
    copies = functools.partial(_page_copies, pt_ref, ckv_hbm.at[layer], kpe_hbm.at[layer], ckv_buf, kpe_buf, sems,
                               n_chunk=n_chunk, pc=pc, page=page)

    @pl.when(s == 0)
    def _():
        for cp in copies(s, slot):
            cp.start()

    @pl.when(s + 1 < total)
    def _():
        for cp in copies(s + 1, 1 - slot):
            cp.start()

    for cp in copies(s, slot):
        cp.wait()

    ql = ql_ref[0]
    qp = qp_ref[0]
    c = s % n_chunk

    @pl.when(c == 0)
    def _():
        cn = cn_ref[0]
        s_self = (jnp.sum(ql * cn, axis=-1, keepdims=True) + jnp.sum(qp * kn_ref[0], axis=-1, keepdims=True)) * SM_SCALE
        m_scr[...] = s_self
        l_scr[...] = jnp.ones_like(l_scr)
        acc_scr[...] = jnp.broadcast_to(cn, acc_scr.shape)

    ckv = ckv_buf[slot]
    sc = (_dot_nt(ql, ckv) + _dot_nt(qp, kpe_buf[slot])) * SM_SCALE
    m_old = m_scr[...]
    m_new = jnp.maximum(m_old, jnp.max(sc, axis=-1, keepdims=True))
    alpha = jnp.exp(m_old - m_new)
    p = jnp.exp(sc - m_new)
    l_scr[...] = alpha * l_scr[...] + jnp.sum(p, axis=-1, keepdims=True)
    acc_scr[...] = alpha * acc_scr[...] + _dot(p, ckv)
    m_scr[...] = m_new

    @pl.when(c == n_chunk - 1)
    def _():
        o_ref[0] = acc_scr[...] / l_scr[...]


def _attn_decode(page_table, q_lat, q_pe, ckv_new, kpe_new, pool_ckv, pool_kpe, layer, pages_per_step=32):
    db, n_pages = page_table.shape
    page = pool_ckv.shape[2]
    pc = _row_tile(n_pages, pages_per_step)
    n_chunk = n_pages // pc
    per_row = lambda s, pt: (s // n_chunk, 0, 0)
    grid_spec = pltpu.PrefetchScalarGridSpec(
        num_scalar_prefetch=1,
        grid=(db * n_chunk,),
        in_specs=[
            pl.BlockSpec((1, A_HEADS, A_KV_LORA), per_row), pl.BlockSpec((1, A_HEADS, A_ROPE), per_row),
            pl.BlockSpec((1, 1, A_KV_LORA), per_row), pl.BlockSpec((1, 1, A_ROPE), per_row),
            pl.BlockSpec(memory_space=pl.ANY), pl.BlockSpec(memory_space=pl.ANY),
        ],
        out_specs=pl.BlockSpec((1, A_HEADS, A_KV_LORA), per_row),
        scratch_shapes=[
            pltpu.VMEM((2, pc * page, A_KV_LORA), F32), pltpu.VMEM((2, pc * page, A_ROPE), F32),
            pltpu.SemaphoreType.DMA((2, 2)),
            pltpu.VMEM((A_HEADS, 1), F32), pltpu.VMEM((A_HEADS, 1), F32), pltpu.VMEM((A_HEADS, A_KV_LORA), F32),
        ],
    )
    return pl.pallas_call(
        functools.partial(_attn_decode_body, layer=layer, n_chunk=n_chunk, pc=pc, page=page),
        grid_spec=grid_spec,
        out_shape=jax.ShapeDtypeStruct((db, A_HEADS, A_KV_LORA), F32),
        compiler_params=_params(("arbitrary",)),
        name="mla_paged_decode",
    )(page_table, q_lat, q_pe, ckv_new, kpe_new, pool_ckv, pool_kpe)


def _cmix_prompt_body(x_ref, g_ref, win_ref, wc_ref, wout_ref, o_ref, buf_ref, carry, *, n_tiles):
    t = pl.program_id(1)
    tm = x_ref.shape[1]
    cw = wout_ref.shape[0]

    @pl.when(t == 0)
    def _():
        carry[...] = jnp.zeros_like(carry)

    x = x_ref[0]
    h = _rms(x, g_ref[...]).astype(BF16)
    p = _dot(h, win_ref[...])
    v = p[:, cw:2 * cw] * p[:, 2 * cw:]
    row = lax.broadcasted_iota(jnp.int32, v.shape, 0)
    prev1 = jnp.broadcast_to(carry[1:2, :], v.shape)
    prev2 = jnp.broadcast_to(carry[0:1, :], v.shape)
    v1 = jnp.where(row == 0, prev1, pltpu.roll(v, 1, 0))
    v2 = jnp.where(row == 0, prev2, jnp.where(row == 1, prev1, pltpu.roll(v, 2, 0)))
    y = wc_ref[0:1, :] * v2 + wc_ref[1:2, :] * v1 + wc_ref[2:3, :] * v
    o_ref[0] = x + _dot((p[:, :cw] * y).astype(BF16), wout_ref[...])
    carry[0:2, :] = v[tm - 2:, :]

    @pl.when(t == n_tiles - 1)
    def _():
        buf_ref[0] = v[tm - 2:, :]


def _cmix_prompt(x, g, w_in, w_conv, w_out, tm=512):
    b, s, d = x.shape
    tm = _row_tile(s, tm)
    n_tiles = s // tm
    cw = w_out.shape[0]
    blk = lambda i, t: (i, t, 0)
    cst = lambda shape: pl.BlockSpec(shape, lambda i, t: (0,) * len(shape), pipeline_mode=pl.Buffered(1))
    return pl.pallas_call(
        functools.partial(_cmix_prompt_body, n_tiles=n_tiles),
        grid=(b, n_tiles),
        in_specs=[pl.BlockSpec((1, tm, d), blk), cst((1, d)), cst(w_in.shape), cst(w_conv.shape), cst(w_out.shape)],
        out_specs=[pl.BlockSpec((1, tm, d), blk), pl.BlockSpec((1, C_KERNEL - 1, cw), lambda i, t: (i, 0, 0))],
        out_shape=[jax.ShapeDtypeStruct((b, s, d), F32), jax.ShapeDtypeStruct((b, C_KERNEL - 1, cw), F32)],
        scratch_shapes=[pltpu.VMEM((8, cw), F32)],
        compiler_params=_params(("parallel", "arbitrary")),
        name="conv_mixer_prompt",
    )(x, g.reshape(1, d), w_in, w_conv, w_out)


def _cmix_step_body(x_ref, b0_ref, b1_ref, g_ref, win_ref, wc_ref, wout_ref, o_ref, v_ref):
    cw = wout_ref.shape[0]
    x = x_ref[...]
    h = _rms(x, g_ref[...]).astype(BF16)
    p = _dot(h, win_ref[...])
    v = p[:, cw:2 * cw] * p[:, 2 * cw:]
    y = wc_ref[0:1, :] * b0_ref[...] + wc_ref[1:2, :] * b1_ref[...] + wc_ref[2:3, :] * v
    o_ref[...] = x + _dot((p[:, :cw] * y).astype(BF16), wout_ref[...])
    v_ref[...] = v


def _cmix_step(x, buf0, buf1, g, w_in, w_conv, w_out):
    n, d = x.shape
    cw = w_out.shape[0]
    args = (x, buf0, buf1, g.reshape(1, d), w_in, w_conv, w_out)
    return pl.pallas_call(
        _cmix_step_body,
        grid=(1,),
        in_specs=[_const_spec(a.shape) for a in args],
        out_specs=[pl.BlockSpec((n, d), lambda i: (0, 0)), pl.BlockSpec((n, cw), lambda i: (0, 0))],
        out_shape=[jax.ShapeDtypeStruct((n, d), F32), jax.ShapeDtypeStruct((n, cw), F32)],
        compiler_params=_params(("arbitrary",)),
        name="conv_mixer_step",
    )(*args)


def _rot_half_cols(w):
    half = A_ROPE // 2
    return jnp.concatenate([-w[..., half:], w[..., :half]], axis=-1)


def _prep_ab_weights(w_in, g_q, g_kv, w_uq, w_ukv, w_out):
    d = w_in.shape[0]
    o_kv = A_Q_LORA
    o_pe = o_kv + A_KV_LORA
    o_h = o_pe + A_ROPE
    k_pe = w_in[:, o_pe:o_h]
    zn = jnp.zeros((d, A_NOPE), F32)
    zt = jnp.zeros((d, HEAD_SLOT - A_NOPE - A_ROPE), F32)
    w_in_p = jnp.concatenate(
        [w_in[:, :o_pe], zn, k_pe, zt, zn, _rot_half_cols(k_pe), zt, w_in[:, o_h:]], axis=1).astype(BF16)

    wq = w_uq.reshape(A_Q_LORA, A_HEADS, A_NOPE + A_ROPE)
    pe = wq[..., A_NOPE:]
    zq = jnp.zeros((A_Q_LORA, A_HEADS, HEAD_SLOT - A_NOPE - A_ROPE), F32)
    q_plain = jnp.concatenate([wq, zq], axis=-1).reshape(A_Q_LORA, A_HEADS * HEAD_SLOT)
    q_rot = jnp.concatenate([jnp.zeros_like(wq[..., :A_NOPE]), _rot_half_cols(pe), zq], axis=-1)
    w_q = jnp.concatenate([q_plain, q_rot.reshape(A_Q_LORA, A_HEADS * HEAD_SLOT)], axis=1).astype(BF16)

    wkv = w_ukv.reshape(A_KV_LORA, A_HEADS, A_NOPE + A_V)
    w_uk, w_uv = wkv[..., :A_NOPE], wkv[..., A_NOPE:]
    k_pad = jnp.concatenate([w_uk, jnp.zeros((A_KV_LORA, A_HEADS, HEAD_SLOT - A_NOPE), F32)], axis=-1)
    w_kv = jnp.concatenate(
        [k_pad.reshape(A_KV_LORA, A_HEADS * HEAD_SLOT), w_uv.reshape(A_KV_LORA, A_HEADS * A_V)], axis=1).astype(BF16)

    eye = jnp.eye(A_HEADS, dtype=F32)
    uk_t = jnp.transpose(w_uk, (1, 2, 0))
    uk_t = jnp.concatenate([uk_t, jnp.zeros((A_HEADS, HEAD_SLOT - A_NOPE, A_KV_LORA), F32)], axis=1)
    absorb = (uk_t[:, :, None, :] * eye[:, None, :, None]).reshape(A_HEADS * HEAD_SLOT, A_HEADS * A_KV_LORA)
    sel = jnp.zeros((HEAD_SLOT, A_ROPE), F32).at[A_NOPE + jnp.arange(A_ROPE), jnp.arange(A_ROPE)].set(1.0)
    sel = (sel[None, :, None, :] * eye[:, None, :, None]).reshape(A_HEADS * HEAD_SLOT, A_HEADS * A_ROPE)
    w_dec_q = jnp.concatenate([absorb, sel], axis=1).astype(BF16)
    uv_t = jnp.transpose(w_uv, (1, 0, 2))
    w_dec_o = (uv_t[:, :, None, :] * eye[:, None, :, None]).reshape(A_HEADS * A_KV_LORA, A_HEADS * A_V).astype(BF16)

    n_a = A_HEADS * A_V
    return dict(w_in=w_in_p, g_q=g_q.reshape(1, -1), g_kv=g_kv.reshape(1, -1), w_q=w_q, w_kv=w_kv,
                w_dec_q=w_dec_q, w_dec_o=w_dec_o, w_out_a=w_out[:n_a].astype(BF16), w_out_b=w_out[n_a:].astype(BF16))


def _rope_tables(pos):
    inv = ROPE_BASE ** (-jnp.arange(0, A_ROPE, 2, dtype=F32) / A_ROPE)
    ang = pos.astype(F32)[:, None] * inv[None, :]
    cos, sin = jnp.cos(ang), jnp.sin(ang)
    n = pos.shape[0]
    tail = jnp.zeros((n, HEAD_SLOT - A_NOPE - A_ROPE), F32)
    cos_t = jnp.concatenate([jnp.ones((n, A_NOPE), F32), cos, cos, tail], axis=1)
    sin_t = jnp.concatenate([jnp.zeros((n, A_NOPE), F32), sin, sin, tail], axis=1)
    return cos_t, sin_t


def kernel(x_prompt, x_sample, cache_ckv, cache_kpe, state_hgrn, state_conv, page_table, norm_g, final_norm_g,
           w_ffn_gate, w_ffn_up, w_ffn_down, w_in_ab, g_q_lora, g_kv_lora, w_uq, w_ukv, hgrn_lower_bound,
           g_hgrn_out, w_out_ab, w_in_c, w_conv_c, w_out_c):
    b_p, s_p, d = x_prompt.shape
    d_b, s_d, _ = x_sample.shape
    assert s_d == 1
    depth = norm_g.shape[0]
    past_len = page_table.shape[1] * cache_ckv.shape[2]
    n_p = b_p * s_p

    cos_p, sin_p = _rope_tables(jnp.arange(s_p))
    cos_d, sin_d = _rope_tables(jnp.full((d_b,), past_len))

    wg = w_ffn_gate.astype(BF16)
    wu = w_ffn_up.astype(BF16)
    wd = w_ffn_down.astype(BF16)

    xp = x_prompt.reshape(n_p, d)
    xd = x_sample.reshape(d_b, d)
    outs = {k: [] for k in ("ckv_p", "kpe_p", "ckv_d", "kpe_d", "hg_p", "hg_d", "cv_p", "cv_d")}
    rope_lanes = slice(A_NOPE, A_NOPE + A_ROPE)

    for layer in range(depth):
        j = layer // 2
        last = layer == depth - 1
        xp = _ffn(xp, norm_g[layer, 0], wg[layer, 0], wu[layer, 0], wd[layer, 0])
        xd = _ffn(xd, norm_g[layer, 0], wg[layer, 0], wu[layer, 0], wd[layer, 0])
        ffn_b = (norm_g[layer, 2], wg[layer, 1], wu[layer, 1], wd[layer, 1])
        fin = final_norm_g if last else None
        if layer % 2 == 0:
            w = _prep_ab_weights(w_in_ab[j], g_q_lora[j], g_kv_lora[j], w_uq[j], w_ukv[j], w_out_ab[j])
            ckv, kpe, q, k, v, hq, hf, hi, hg = _ab_pre(xp, norm_g[layer, 1], w, cos_p, sin_p, hgrn_lower_bound, layer)
            o_a = _attn_prompt(q.reshape(b_p, s_p, -1), k.reshape(b_p, s_p, -1), v.reshape(b_p, s_p, -1))
            sh = (b_p, s_p, -1)
            o_b, st_p = _hgrn_prompt(hq.reshape(sh), hf.reshape(sh), hi.reshape(sh), hg.reshape(sh), g_hgrn_out[j])
            pre = ((o_a.reshape(n_p, -1), w["w_out_a"]), (o_b.reshape(n_p, -1), w["w_out_b"]))
            xp = _ffn(xp, *ffn_b, pre=pre, final_g=fin)
            outs["ckv_p"].append(ckv.reshape(b_p, s_p, -1))
            outs["kpe_p"].append(kpe[:, rope_lanes].reshape(b_p, s_p, -1))
            outs["hg_p"].append(st_p)
            ckv, kpe, q, _, _, hq, hf, hi, hg = _ab_pre(xd, norm_g[layer, 1], w, cos_d, sin_d, hgrn_lower_bound, layer)
            kpe = kpe[:, rope_lanes]
            qd = _small_matmul(q, w["w_dec_q"], F32)
            n_lat = A_HEADS * A_KV_LORA
            o_lat = _attn_decode(page_table, qd[:, :n_lat].reshape(d_b, A_HEADS, A_KV_LORA),
                                 qd[:, n_lat:].reshape(d_b, A_HEADS, A_ROPE), ckv.reshape(d_b, 1, -1),
                                 kpe.reshape(d_b, 1, -1), cache_ckv, cache_kpe, j)
            o_a = _small_matmul(o_lat.reshape(d_b, n_lat), w["w_dec_o"], BF16)
            o_b, st_d = _hgrn_step(hq, hf, hi, hg, g_hgrn_out[j], state_hgrn, j)
            xd = _ffn(xd, *ffn_b, pre=((o_a, w["w_out_a"]), (o_b, w["w_out_b"])), final_g=fin)
            outs["ckv_d"].append(ckv)
            outs["kpe_d"].append(kpe)
            outs["hg_d"].append(st_d)
        else:
            win = w_in_c[j].astype(BF16)
            wout = w_out_c[j].astype(BF16)
            xp3, buf_p = _cmix_prompt(xp.reshape(b_p, s_p, d), norm_g[layer, 1], win, w_conv_c[j], wout)
            xp = _ffn(xp3.reshape(n_p, d), *ffn_b, final_g=fin)
            xd, v_new = _cmix_step(xd, state_conv[j, :, 0], state_conv[j, :, 1], norm_g[layer, 1], win, w_conv_c[j], wout)
            xd = _ffn(xd, *ffn_b, final_g=fin)
            outs["cv_p"].append(buf_p)
            outs["cv_d"].append(jnp.stack([state_conv[j, :, 1], v_new], axis=1))

    return (xp.reshape(b_p, s_p, d), xd.reshape(d_b, s_d, d),
            jnp.stack(outs["ckv_p"]), jnp.stack(outs["kpe_p"]),
            jnp.stack(outs["ckv_d"]).reshape(-1, d_b, s_d, A_KV_LORA), jnp.stack(outs["kpe_d"]).reshape(-1, d_b, s_d, A_ROPE),
            jnp.stack(outs["hg_p"]), jnp.stack(outs["hg_d"]), jnp.stack(outs["cv_p"]), jnp.stack(outs["cv_d"]))
```

```python
import functools

import jax
import jax.numpy as jnp
from jax import lax
from jax.experimental import pallas as pl
from jax.experimental.pallas import tpu as pltpu

F32 = jnp.float32
BF16 = jnp.bfloat16

A_HEADS = 8
A_NOPE = 64
A_ROPE = 32
A_V = 64
A_Q_LORA = 384
A_KV_LORA = 256
ROPE_BASE = 10000.0
SM_SCALE = (A_NOPE + A_ROPE) ** -0.5
LOG2_E = 1.4426950408889634
B_HEADS = 4
B_DK = 128
B_DV = 128
B_CHUNK = 16
C_KERNEL = 3
RMS_EPS = 1e-6

LANES = 128
HEAD_SLOT = LANES
HGRN_BLOCK = 128
VMEM_LIMIT = 56 * 1024 * 1024
DECODE_BUFFER_BYTES = 40 * 1024 * 1024


def _params(sem, vmem=VMEM_LIMIT):
    return pltpu.CompilerParams(dimension_semantics=sem, vmem_limit_bytes=vmem)


def _const_spec(shape):
    nd = len(shape)
    return pl.BlockSpec(shape, lambda *_: (0,) * nd, pipeline_mode=pl.Buffered(1))


def _rms(x, g):
    return x * lax.rsqrt(jnp.mean(x * x, axis=-1, keepdims=True) + RMS_EPS) * g


def _dot(a, b):
    return jnp.dot(a, b, preferred_element_type=F32)


def _dot_nt(a, b):
    return lax.dot_general(a, b, (((1,), (1,)), ((), ())), preferred_element_type=F32)


def _dot_tn(a, b):
    return lax.dot_general(a, b, (((0,), (0,)), ((), ())), preferred_element_type=F32)


def _silu(x):
    return x * jax.nn.sigmoid(x)


def _row_tile(n, want):
    t = min(n, want)
    assert n % t == 0, (n, t)
    return t


def _ffn_body(*refs, n_chunks, tf, n_pre, has_final):
    x_ref = refs[0]
    pre = refs[1:1 + 2 * n_pre]
    g_ref, wg_ref, wu_ref, wd_ref = refs[1 + 2 * n_pre:5 + 2 * n_pre]
    rest = refs[5 + 2 * n_pre:]
    gf_ref = rest[0] if has_final else None
    o_ref = rest[-1]

    x = x_ref[...]
    for p in range(n_pre):
        x = x + _dot(pre[2 * p][...], pre[2 * p + 1][...])
    h = _rms(x, g_ref[...]).astype(BF16)
    acc = None
    for c in range(n_chunks):
        cs = slice(c * tf, (c + 1) * tf)
        gate = _dot(h, wg_ref[:, cs])
        up = _dot(h, wu_ref[:, cs])
        a = (_silu(gate) * up).astype(BF16)
        part = _dot(a, wd_ref[cs, :])
        acc = part if acc is None else acc + part
    y = x + 0.5 * acc
    if has_final:
        y = _rms(y, gf_ref[...])
    o_ref[...] = y


def _ffn(x, g, wg, wu, wd, pre=(), final_g=None, tm=1024):
    n, d = x.shape
    f = wg.shape[1]
    tf = 256 if f % 256 == 0 else LANES
    tm = _row_tile(n, tm)
    row = lambda i: (i, 0)
    in_specs = [pl.BlockSpec((tm, d), row)]
    args = [x]
    for act, w in pre:
        in_specs += [pl.BlockSpec((tm, act.shape[1]), row), _const_spec(w.shape)]
        args += [act, w]
    in_specs += [_const_spec((1, d)), _const_spec(wg.shape), _const_spec(wu.shape), _const_spec(wd.shape)]
    args += [g.reshape(1, d), wg, wu, wd]
    if final_g is not None:
        in_specs.append(_const_spec((1, d)))
        args.append(final_g.reshape(1, d))
    body = functools.partial(_ffn_body, n_chunks=f // tf, tf=tf, n_pre=len(pre), has_final=final_g is not None)
    return pl.pallas_call(
        body,
        grid=(n // tm,),
        in_specs=in_specs,
        out_specs=pl.BlockSpec((tm, d), row),
        out_shape=jax.ShapeDtypeStruct((n, d), F32),
        compiler_params=_params(("parallel",)),
        name="half_ffn",
    )(*args)


def _ab_pre_body(x_ref, g_ref, win_ref, gq_ref, gkv_ref, wq_ref, wkv_ref, cosq_ref, sinq_ref, cos_ref, sin_ref, lb_ref,
                 ckv_ref, kpet_ref, q_ref, k_ref, v_ref, hq_ref, hf_ref, hi_ref, hg_ref, *, layer):
    hb = B_HEADS * B_DK
    h = _rms(x_ref[...], g_ref[...]).astype(BF16)
    proj = _dot(h, win_ref[...])
    o_kv = A_Q_LORA
    o_pa = o_kv + A_KV_LORA
    o_pb = o_pa + HEAD_SLOT
    o_h = o_pb + HEAD_SLOT
    cos = cos_ref[...]
    sin = sin_ref[...]
    cos_q = cosq_ref[...]
    sin_q = sinq_ref[...]

    cq = _rms(proj[:, :A_Q_LORA], gq_ref[...]).astype(BF16)
    qq = _dot(cq, wq_ref[...])
    hw = A_HEADS * HEAD_SLOT
    for hd in range(A_HEADS):
        sl = slice(hd * HEAD_SLOT, (hd + 1) * HEAD_SLOT)
        sl2 = slice(hw + hd * HEAD_SLOT, hw + (hd + 1) * HEAD_SLOT)
        q_ref[:, sl] = (qq[:, sl] * cos_q + qq[:, sl2] * sin_q).astype(BF16)

    ckv = _rms(proj[:, o_kv:o_pa], gkv_ref[...])
    ckv_ref[...] = ckv
    kpe = proj[:, o_pa:o_pb] * cos + proj[:, o_pb:o_h] * sin
    kpet_ref[0] = kpe.T[A_NOPE:A_NOPE + A_ROPE, :]
    kv = _dot(ckv.astype(BF16), wkv_ref[...])
    for hd in range(A_HEADS):
        sl = slice(hd * HEAD_SLOT, (hd + 1) * HEAD_SLOT)
        k_ref[:, sl] = (kv[:, sl] + kpe).astype(BF16)
    v_ref[...] = kv[:, hw:].astype(BF16)

    lbr = lb_ref[...]
    e = jnp.exp(lbr - jnp.max(lbr, axis=0, keepdims=True))
    lb = jnp.sum(e[:layer + 1], axis=0, keepdims=True) / jnp.sum(e, axis=0, keepdims=True)
    hq_ref[...] = _silu(proj[:, o_h:o_h + hb]) * (B_DK ** -0.5)
    hf_ref[...] = lb + (1.0 - lb) * jax.nn.sigmoid(proj[:, o_h + hb:o_h + 2 * hb])
    hi_ref[...] = proj[:, o_h + 2 * hb:o_h + 3 * hb].astype(BF16)
    hg_ref[...] = _silu(proj[:, o_h + 3 * hb:o_h + 4 * hb])


def _ab_pre(x, g, w, tables, lb_raw, layer, tm=512):
    n, d = x.shape
    s_len = tables[0].shape[0]
    tm = _row_tile(s_len, tm)
    assert n % s_len == 0
    n_pos_tiles = s_len // tm
    row = lambda i: (i, 0)
    pos = lambda i: (i % n_pos_tiles, 0)
    hw = A_HEADS * HEAD_SLOT
    hb = B_HEADS * B_DK
    outs = [
        (A_KV_LORA, F32), None, (hw, BF16), (hw, BF16), (A_HEADS * A_V, BF16),
        (hb, F32), (hb, F32), (hb, BF16), (hb, F32),
    ]
    kpet_spec = pl.BlockSpec((1, A_ROPE, tm), lambda i: (i // n_pos_tiles, 0, i % n_pos_tiles))
    kpet_shape = jax.ShapeDtypeStruct((n // s_len, A_ROPE, s_len), F32)
    return pl.pallas_call(
        functools.partial(_ab_pre_body, layer=layer),
        grid=(n // tm,),
        in_specs=[
            pl.BlockSpec((tm, d), row), _const_spec((1, d)), _const_spec(w["w_in"].shape),
            _const_spec((1, A_Q_LORA)), _const_spec((1, A_KV_LORA)), _const_spec(w["w_q"].shape),
            _const_spec(w["w_kv"].shape)] + [pl.BlockSpec((tm, HEAD_SLOT), pos)] * 4 + [_const_spec(lb_raw.shape)],
        out_specs=[kpet_spec if o is None else pl.BlockSpec((tm, o[0]), row) for o in outs],
        out_shape=[kpet_shape if o is None else jax.ShapeDtypeStruct((n, o[0]), o[1]) for o in outs],
        compiler_params=_params(("parallel",)),
        name="ab_input_stage",
    )(x, g.reshape(1, d), w["w_in"], w["g_q"], w["g_kv"], w["w_q"], w["w_kv"], *tables, lb_raw)


def _attn_prompt_body(q_ref, k_ref, v_ref, o_ref, vaug, *, tq):
    s_len = q_ref.shape[1]
    nq = s_len // tq
    nv = 2 * A_V
    lane = lax.broadcasted_iota(jnp.int32, (tq, nv), 1)
    r_i = lax.broadcasted_iota(jnp.int32, (tq, tq), 0)
    c_i = lax.broadcasted_iota(jnp.int32, (tq, tq), 1)
    causal = c_i <= r_i
    vaug[:, :nv] = v_ref[0]
    vaug[:, nv:] = jnp.ones((s_len, nv), BF16)
    for qi in range(nq):
        rows = slice(qi * tq, (qi + 1) * tq)
        n_past = qi * tq
        outs = []
        for hh in range(2):
            hs = slice(hh * HEAD_SLOT, (hh + 1) * HEAD_SLOT)
            q = q_ref[0, rows, hs]
            s_d = jnp.where(causal, _dot_nt(q, k_ref[0, rows, hs]), -jnp.inf)
            m = jnp.max(s_d, axis=-1, keepdims=True)
            if n_past:
                s_p = _dot_nt(q, k_ref[0, :n_past, hs])
                m = jnp.maximum(m, jnp.max(s_p, axis=-1, keepdims=True))
            o = _dot(jnp.exp2(s_d - m).astype(BF16), vaug[rows, :])
            if n_past:
                o = o + _dot(jnp.exp2(s_p - m).astype(BF16), vaug[:n_past, :])
            outs.append(o[:, :nv] / o[:, nv:])
        o_ref[0, rows, :] = jnp.where(lane < A_V, outs[0], outs[1]).astype(BF16)


def _attn_prompt(q, k, v, tq=256):
    b, s, _ = q.shape
    tq = _row_tile(s, tq)
    blk2 = lambda i, j: (i, 0, j)
    return pl.pallas_call(
        functools.partial(_attn_prompt_body, tq=tq),
        grid=(b, A_HEADS // 2),
        in_specs=[
            pl.BlockSpec((1, s, 2 * HEAD_SLOT), blk2), pl.BlockSpec((1, s, 2 * HEAD_SLOT), blk2),
            pl.BlockSpec((1, s, 2 * A_V), blk2),
        ],
        out_specs=pl.BlockSpec((1, s, 2 * A_V), blk2),
        out_shape=jax.ShapeDtypeStruct((b, s, A_HEADS * A_V), BF16),
        scratch_shapes=[pltpu.VMEM((s, 4 * A_V), BF16)],
        compiler_params=_params(("parallel", "parallel")),
        name="mla_prompt_attention",
    )(q, k, v)


def _hgrn_prompt_body(q_ref, f_ref, i_ref, gt_ref, go_ref, o_ref, st_ref, st_scr, *, n_sb):
    R = HGRN_BLOCK
    hb = B_HEADS * B_DK
    t = pl.program_id(1)

    @pl.when(t == 0)
    def _():
        st_scr[...] = jnp.zeros_like(st_scr)

    r_i = lax.broadcasted_iota(jnp.int32, (R, R), 0)
    c_i = lax.broadcasted_iota(jnp.int32, (R, R), 1)
    tri = jnp.where(c_i <= r_i, 1.0, 0.0).astype(BF16)
    base_mask = ((r_i >> 4) == (c_i >> 4)) & (c_i <= r_i)
    levels = []
    m = B_CHUNK
    while 2 * m <= R:
        sh = (2 * m).bit_length() - 1
        levels.append((m, ((r_i >> sh) == (c_i >> sh)) & ((r_i & (2 * m - 1)) >= m) & ((c_i & (2 * m - 1)) < m)))
        m *= 2
    go = go_ref[...]

    for sb in range(n_sb):
        rows = slice(sb * R, (sb + 1) * R)
        q = q_ref[0, rows, :]
        f = f_ref[0, rows, :]
        v = i_ref[0, rows, :]
        k = 1.0 - f
        lf = jnp.log(f)
        hi = lf.astype(BF16)
        r1 = lf - hi.astype(F32)
        mid = r1.astype(BF16)
        lo = (r1 - mid.astype(F32)).astype(BF16)
        G = _dot(tri, hi) + _dot(tri, mid) + _dot(tri, lo)
        g_last = G[R - 1:R, :]

        n_ch = R // B_CHUNK
        end_rows = [G[B_CHUNK * (j + 1) - 1:B_CHUNK * (j + 1), :] for j in range(n_ch)]
        g_end = jnp.concatenate(end_rows, axis=0)
        g_start = jnp.concatenate([jnp.zeros((1, hb), F32)] + end_rows[:-1], axis=0)

        def expand(per_chunk):
            return jnp.concatenate([jnp.broadcast_to(per_chunk[j:j + 1, :], (B_CHUNK, hb)) for j in range(n_ch)], axis=0)

        g0 = expand(g_start)
        q0 = q * jnp.exp(G - g0)
        k_end = k * jnp.exp(expand(g_end) - G)
        q_lv = [q0.astype(BF16)]
        k_lv = [(k * jnp.exp(g0 - G)).astype(BF16)]
        q_int = (q0 * expand(jnp.exp(g_start))).astype(BF16)
        k_st = (k_end * expand(jnp.exp(g_last - g_end))).astype(BF16)
        for mm, _ in levels:
            cpb = 2 * mm // B_CHUNK
            g_mid = jnp.concatenate([end_rows[(j // cpb) * cpb + cpb // 2 - 1] for j in range(n_ch)], axis=0)
            q_lv.append((q0 * expand(jnp.exp(jnp.minimum(g_start - g_mid, 0.0)))).astype(BF16))
            k_lv.append((k_end * expand(jnp.exp(jnp.minimum(g_mid - g_end, 0.0)))).astype(BF16))
        decay = jnp.exp(g_last)

        for hd in range(B_HEADS):
            hs = slice(hd * B_DK, (hd + 1) * B_DK)
            att = jnp.where(base_mask, _dot_nt(q_lv[0][:, hs], k_lv[0][:, hs]), 0.0)
            for li, (_, msk) in enumerate(levels):
                att = jnp.where(msk, _dot_nt(q_lv[li + 1][:, hs], k_lv[li + 1][:, hs]), att)
            st = st_scr[hd]
            o = _dot(att.astype(BF16), v[:, hs]) + _dot_nt(q_int[:, hs], st.astype(BF16))
            st_scr[hd] = st * decay[:, hs] + _dot_tn(v[:, hs], k_st[:, hs])
            o = o * lax.rsqrt(jnp.mean(o * o, axis=-1, keepdims=True) + RMS_EPS) * go
            o_ref[0, rows, hs] = (o * gt_ref[0, rows, hs]).astype(BF16)

    @pl.when(t == pl.num_programs(1) - 1)
    def _():
        for hd in range(B_HEADS):
            st_ref[0, hd] = st_scr[hd].T


def _hgrn_prompt(hq, hf, hi, hg, g_out, rows_per_step=256):
    b, s, hb = hq.shape
    assert s % HGRN_BLOCK == 0
    ts = _row_tile(s, rows_per_step)
    blk = lambda i, t: (i, t, 0)
    return pl.pallas_call(
        functools.partial(_hgrn_prompt_body, n_sb=ts // HGRN_BLOCK),
        grid=(b, s // ts),
        in_specs=[pl.BlockSpec((1, ts, hb), blk)] * 4 + [pl.BlockSpec((1, B_DV), lambda i, t: (0, 0))],
        out_specs=[
            pl.BlockSpec((1, ts, hb), blk),
            pl.BlockSpec((1, B_HEADS, B_DK, B_DV), lambda i, t: (i, 0, 0, 0)),
        ],
        out_shape=[
            jax.ShapeDtypeStruct((b, s, hb), BF16),
            jax.ShapeDtypeStruct((b, B_HEADS, B_DK, B_DV), F32),
        ],
        scratch_shapes=[pltpu.VMEM((B_HEADS, B_DV, B_DK), F32)],
        compiler_params=_params(("parallel", "arbitrary")),
        name="hgrn2_prompt",
    )(hq, hf, hi, hg, g_out.reshape(1, B_DV))


def _hgrn_step_body(q_ref, f_ref, i_ref, gt_ref, go_ref, s_ref, o_ref, so_ref, *, nb):
    go = go_ref[...]
    pad = jnp.zeros((B_DK - 3 * nb, B_DK), F32)
    for hd in range(B_HEADS):
        hs = slice(hd * B_DK, (hd + 1) * B_DK)
        f = f_ref[:, hs]
        cols = jnp.concatenate([q_ref[:, hs], f, 1.0 - f, pad], axis=0).T
        for j in range(nb):
            s_new = cols[:, nb + j:nb + j + 1] * s_ref[j, hd] + cols[:, 2 * nb + j:2 * nb + j + 1] * i_ref[j:j + 1, hs].astype(F32)
            so_ref[j, hd] = s_new
            o = jnp.sum(cols[:, j:j + 1] * s_new, axis=0, keepdims=True)
            o = o * lax.rsqrt(jnp.mean(o * o, axis=-1, keepdims=True) + RMS_EPS) * go
            o_ref[j:j + 1, hs] = (o * gt_ref[j:j + 1, hs]).astype(BF16)


def _hgrn_step(hq, hf, hi, hg, g_out, state, layer, nb=8):
    n, hb = hq.shape
    nb = _row_tile(n, nb)
    row = lambda i: (i, 0)
    st_in = lambda i: (layer, i, 0, 0, 0)
    st_out = lambda i: (i, 0, 0, 0)
    return pl.pallas_call(
        functools.partial(_hgrn_step_body, nb=nb),
        grid=(n // nb,),
        in_specs=[pl.BlockSpec((nb, hb), row)] * 4 + [_const_spec((1, B_DV)), pl.BlockSpec((None, nb, B_HEADS, B_DK, B_DV), st_in)],
        out_specs=[pl.BlockSpec((nb, hb), row), pl.BlockSpec((nb, B_HEADS, B_DK, B_DV), st_out)],
        out_shape=[jax.ShapeDtypeStruct((n, hb), BF16), jax.ShapeDtypeStruct(state.shape[1:], F32)],
        compiler_params=_params(("parallel",)),
        name="hgrn2_step",
    )(hq, hf, hi, hg, g_out.reshape(1, B_DV), state)


def _small_matmul_body(a_ref, w_ref, o_ref):
    o_ref[...] = _dot(a_ref[...].astype(BF16), w_ref[...]).astype(o_ref.dtype)


def _small_matmul(a, w, out_dtype):
    n = a.shape[0]
    out_spec = pl.BlockSpec((n, w.shape[1]), lambda i: (0, 0))
    return pl.pallas_call(
        _small_matmul_body,
        grid=(1,),
        in_specs=[_const_spec(a.shape), _const_spec(w.shape)],
        out_specs=out_spec,
        out_shape=jax.ShapeDtypeStruct((n, w.shape[1]), out_dtype),
        compiler_params=_params(("arbitrary",)),
        name="sample_matmul",
    )(a, w)


def _page_copies(pt_ref, ckv_hbm, kpe_hbm, ckv_buf, kpe_buf, sems, step, slot, *, n_chunk, pc, page):
    b = step // n_chunk
    c = step % n_chunk
    copies = []
    for p in range(pc):
        pg = pt_ref[b, c * pc + p]
        copies.append(pltpu.make_async_copy(ckv_hbm.at[pg], ckv_buf.at[slot, pl.ds(p * page, page), :], sems.at[slot, 0]))
        copies.append(pltpu.make_async_copy(kpe_hbm.at[pg], kpe_buf.at[slot, :, pl.ds(p * page, page)], sems.at[slot, 1]))
    return copies


def _attn_decode_body(pt_ref, ql_ref, qp_ref, cn_ref, kn_ref, ckv_hbm, kpe_hbm, o_ref,
                      ckv_buf, kpe_buf, sems, m_scr, l_scr, acc_scr, *, layer, n_chunk, pc, page):
    s = pl.program_id(0)
    total = pl.num_programs(0)
    slot = s % 2
    copies = functools.partial(_page_copies, pt_ref, ckv_hbm.at[layer], kpe_hbm.at[layer], ckv_buf, kpe_buf, sems,
                               n_chunk=n_chunk, pc=pc, page=page)

    @pl.when(s == 0)
    def _():
        for cp in copies(s, slot):
            cp.start()

    @pl.when(s + 1 < total)
    def _():
        for cp in copies(s + 1, 1 - slot):
            cp.start()

    for cp in copies(s, slot):
        cp.wait()

    ql = ql_ref[0]
    qp = qp_ref[0]
    c = s % n_chunk

    @pl.when(c == 0)
    def _():
        cn = cn_ref[0]
        s_self = jnp.sum(ql * cn, axis=-1, keepdims=True) + jnp.sum(qp * kn_ref[0], axis=-1, keepdims=True)
        m_scr[...] = s_self
        l_scr[...] = jnp.ones_like(l_scr)
        acc_scr[...] = jnp.broadcast_to(cn, acc_scr.shape)

    ckv = ckv_buf[slot]
    sc = _dot_nt(ql, ckv) + _dot(qp, kpe_buf[slot])
    m_old = m_scr[...]
    m_new = jnp.maximum(m_old, jnp.max(sc, axis=-1, keepdims=True))
    alpha = jnp.exp2(m_old - m_new)
    p = jnp.exp2(sc - m_new)
    l_scr[...] = alpha * l_scr[...] + jnp.sum(p, axis=-1, keepdims=True)
    acc_scr[...] = alpha * acc_scr[...] + _dot(p, ckv)
    m_scr[...] = m_new

    @pl.when(c == n_chunk - 1)
    def _():
        o_ref[0] = acc_scr[...] / l_scr[...]


def _decode_pages_per_step(n_pages, page):
    budget = DECODE_BUFFER_BYTES // (2 * page * (A_KV_LORA + A_ROPE) * 4)
    return max(p for p in range(1, n_pages + 1) if n_pages % p == 0 and p <= max(budget, 1))


def _attn_decode(page_table, q_lat, q_pe, ckv_new, kpe_new, pool_ckv, pool_kpe_t, layer):
    db, n_pages = page_table.shape
    page = pool_ckv.shape[2]
    pc = _decode_pages_per_step(n_pages, page)
    n_chunk = n_pages // pc
    per_row = lambda s, pt: (s // n_chunk, 0, 0)
    grid_spec = pltpu.PrefetchScalarGridSpec(
        num_scalar_prefetch=1,
        grid=(db * n_chunk,),
        in_specs=[
            pl.BlockSpec((1, A_HEADS, A_KV_LORA), per_row), pl.BlockSpec((1, A_HEADS, A_ROPE), per_row),
            pl.BlockSpec((1, 1, A_KV_LORA), per_row), pl.BlockSpec((1, 1, A_ROPE), per_row),
            pl.BlockSpec(memory_space=pl.ANY), pl.BlockSpec(memory_space=pl.ANY),
        ],
        out_specs=pl.BlockSpec((1, A_HEADS, A_KV_LORA), per_row),
        scratch_shapes=[
            pltpu.VMEM((2, pc * page, A_KV_LORA), F32), pltpu.VMEM((2, A_ROPE, pc * page), F32),
            pltpu.SemaphoreType.DMA((2, 2)),
            pltpu.VMEM((A_HEADS, 1), F32), pltpu.VMEM((A_HEADS, 1), F32), pltpu.VMEM((A_HEADS, A_KV_LORA), F32),
        ],
    )
    return pl.pallas_call(
        functools.partial(_attn_decode_body, layer=layer, n_chunk=n_chunk, pc=pc, page=page),
        grid_spec=grid_spec,
        out_shape=jax.ShapeDtypeStruct((db, A_HEADS, A_KV_LORA), F32),
        compiler_params=_params(("arbitrary",)),
        name="mla_paged_decode",
    )(page_table, q_lat, q_pe, ckv_new, kpe_new, pool_ckv, pool_kpe_t)


def _cmix_prompt_body(x_ref, g_ref, win_ref, wc_ref, wout_ref, o_ref, buf_ref, carry, *, n_tiles):
    t = pl.program_id(1)
    tm = x_ref.shape[1]
    cw = wout_ref.shape[0]

    @pl.when(t == 0)
    def _():
        carry[...] = jnp.zeros_like(carry)

    x = x_ref[0]
    h = _rms(x, g_ref[...]).astype(BF16)
    p = _dot(h, win_ref[...])
    v = p[:, cw:2 * cw] * p[:, 2 * cw:]
    row = lax.broadcasted_iota(jnp.int32, v.shape, 0)
    prev1 = jnp.broadcast_to(carry[1:2, :], v.shape)
    prev2 = jnp.broadcast_to(carry[0:1, :], v.shape)
    v1 = jnp.where(row == 0, prev1, pltpu.roll(v, 1, 0))
    v2 = jnp.where(row == 0, prev2, jnp.where(row == 1, prev1, pltpu.roll(v, 2, 0)))
    y = wc_ref[0:1, :] * v2 + wc_ref[1:2, :] * v1 + wc_ref[2:3, :] * v
    o_ref[0] = x + _dot((p[:, :cw] * y).astype(BF16), wout_ref[...])
    carry[0:2, :] = v[tm - 2:, :]

    @pl.when(t == n_tiles - 1)
    def _():
        buf_ref[0] = v[tm - 2:, :]


def _cmix_prompt(x, g, w_in, w_conv, w_out, tm=512):
    b, s, d = x.shape
    tm = _row_tile(s, tm)
    n_tiles = s // tm
    cw = w_out.shape[0]
    blk = lambda i, t: (i, t, 0)
    cst = lambda shape: pl.BlockSpec(shape, lambda i, t: (0,) * len(shape), pipeline_mode=pl.Buffered(1))
    return pl.pallas_call(
        functools.partial(_cmix_prompt_body, n_tiles=n_tiles),
        grid=(b, n_tiles),
        in_specs=[pl.BlockSpec((1, tm, d), blk), cst((1, d)), cst(w_in.shape), cst(w_conv.shape), cst(w_out.shape)],
        out_specs=[pl.BlockSpec((1, tm, d), blk), pl.BlockSpec((1, C_KERNEL - 1, cw), lambda i, t: (i, 0, 0))],
        out_shape=[jax.ShapeDtypeStruct((b, s, d), F32), jax.ShapeDtypeStruct((b, C_KERNEL - 1, cw), F32)],
        scratch_shapes=[pltpu.VMEM((8, cw), F32)],
        compiler_params=_params(("parallel", "arbitrary")),
        name="conv_mixer_prompt",
    )(x, g.reshape(1, d), w_in, w_conv, w_out)


def _cmix_step_body(x_ref, b0_ref, b1_ref, g_ref, win_ref, wc_ref, wout_ref, o_ref, v_ref):
    cw = wout_ref.shape[0]
    x = x_ref[...]
    h = _rms(x, g_ref[...]).astype(BF16)
    p = _dot(h, win_ref[...])
    v = p[:, cw:2 * cw] * p[:, 2 * cw:]
    y = wc_ref[0:1, :] * b0_ref[...] + wc_ref[1:2, :] * b1_ref[...] + wc_ref[2:3, :] * v
    o_ref[...] = x + _dot((p[:, :cw] * y).astype(BF16), wout_ref[...])
    v_ref[...] = v


def _cmix_step(x, buf0, buf1, g, w_in, w_conv, w_out):
    n, d = x.shape
    cw = w_out.shape[0]
    args = (x, buf0, buf1, g.reshape(1, d), w_in, w_conv, w_out)
    return pl.pallas_call(
        _cmix_step_body,
        grid=(1,),
        in_specs=[_const_spec(a.shape) for a in args],
        out_specs=[pl.BlockSpec((n, d), lambda i: (0, 0)), pl.BlockSpec((n, cw), lambda i: (0, 0))],
        out_shape=[jax.ShapeDtypeStruct((n, d), F32), jax.ShapeDtypeStruct((n, cw), F32)],
        compiler_params=_params(("arbitrary",)),
        name="conv_mixer_step",
    )(*args)


def _rot_half_cols(w):
    half = A_ROPE // 2
    return jnp.concatenate([-w[..., half:], w[..., :half]], axis=-1)


def _prep_ab_weights(w_in, g_q, g_kv, w_uq, w_ukv, w_out):
    d = w_in.shape[0]
    o_kv = A_Q_LORA
    o_pe = o_kv + A_KV_LORA
    o_h = o_pe + A_ROPE
    k_pe = w_in[:, o_pe:o_h]
    zn = jnp.zeros((d, A_NOPE), F32)
    zt = jnp.zeros((d, HEAD_SLOT - A_NOPE - A_ROPE), F32)
    w_in_p = jnp.concatenate(
        [w_in[:, :o_pe], zn, k_pe, zt, zn, _rot_half_cols(k_pe), zt, w_in[:, o_h:]], axis=1).astype(BF16)

    wq = w_uq.reshape(A_Q_LORA, A_HEADS, A_NOPE + A_ROPE)
    pe = wq[..., A_NOPE:]
    zq = jnp.zeros((A_Q_LORA, A_HEADS, HEAD_SLOT - A_NOPE - A_ROPE), F32)
    q_plain = jnp.concatenate([wq, zq], axis=-1).reshape(A_Q_LORA, A_HEADS * HEAD_SLOT)
    q_rot = jnp.concatenate([jnp.zeros_like(wq[..., :A_NOPE]), _rot_half_cols(pe), zq], axis=-1)
    w_q = jnp.concatenate([q_plain, q_rot.reshape(A_Q_LORA, A_HEADS * HEAD_SLOT)], axis=1).astype(BF16)

    wkv = w_ukv.reshape(A_KV_LORA, A_HEADS, A_NOPE + A_V)
    w_uk, w_uv = wkv[..., :A_NOPE], wkv[..., A_NOPE:]
    k_pad = jnp.concatenate([w_uk, jnp.zeros((A_KV_LORA, A_HEADS, HEAD_SLOT - A_NOPE), F32)], axis=-1)
    w_kv = jnp.concatenate(
        [k_pad.reshape(A_KV_LORA, A_HEADS * HEAD_SLOT), w_uv.reshape(A_KV_LORA, A_HEADS * A_V)], axis=1).astype(BF16)

    eye = jnp.eye(A_HEADS, dtype=F32)
    uk_t = jnp.transpose(w_uk, (1, 2, 0))
    uk_t = jnp.concatenate([uk_t, jnp.zeros((A_HEADS, HEAD_SLOT - A_NOPE, A_KV_LORA), F32)], axis=1)
    absorb = (uk_t[:, :, None, :] * eye[:, None, :, None]).reshape(A_HEADS * HEAD_SLOT, A_HEADS * A_KV_LORA)
    sel = jnp.zeros((HEAD_SLOT, A_ROPE), F32).at[A_NOPE + jnp.arange(A_ROPE), jnp.arange(A_ROPE)].set(1.0)
    sel = (sel[None, :, None, :] * eye[:, None, :, None]).reshape(A_HEADS * HEAD_SLOT, A_HEADS * A_ROPE)
    w_dec_q = jnp.concatenate([absorb, sel], axis=1).astype(BF16)
    uv_t = jnp.transpose(w_uv, (1, 0, 2))
    w_dec_o = (uv_t[:, :, None, :] * eye[:, None, :, None]).reshape(A_HEADS * A_KV_LORA, A_HEADS * A_V).astype(BF16)

    n_a = A_HEADS * A_V
    return dict(w_in=w_in_p, g_q=g_q.reshape(1, -1), g_kv=g_kv.reshape(1, -1), w_q=w_q, w_kv=w_kv,
                w_dec_q=w_dec_q, w_dec_o=w_dec_o, w_out_a=w_out[:n_a].astype(BF16), w_out_b=w_out[n_a:].astype(BF16))


def _rope_tables(pos):
    inv = ROPE_BASE ** (-jnp.arange(0, A_ROPE, 2, dtype=F32) / A_ROPE)
    ang = pos.astype(F32)[:, None] * inv[None, :]
    cos, sin = jnp.cos(ang), jnp.sin(ang)
    n = pos.shape[0]
    tail = jnp.zeros((n, HEAD_SLOT - A_NOPE - A_ROPE), F32)
    cos_t = jnp.concatenate([jnp.ones((n, A_NOPE), F32), cos, cos, tail], axis=1)
    sin_t = jnp.concatenate([jnp.zeros((n, A_NOPE), F32), sin, sin, tail], axis=1)
    c = SM_SCALE * LOG2_E
    return cos_t * c, sin_t * c, cos_t, sin_t


def kernel(x_prompt, x_sample, cache_ckv, cache_kpe, state_hgrn, state_conv, page_table, norm_g, final_norm_g,
           w_ffn_gate, w_ffn_up, w_ffn_down, w_in_ab, g_q_lora, g_kv_lora, w_uq, w_ukv, hgrn_lower_bound,
           g_hgrn_out, w_out_ab, w_in_c, w_conv_c, w_out_c):
    b_p, s_p, d = x_prompt.shape
    d_b, s_d, _ = x_sample.shape
    assert s_d == 1
    depth = norm_g.shape[0]
    past_len = page_table.shape[1] * cache_ckv.shape[2]
    n_p = b_p * s_p

    tab_p = _rope_tables(jnp.arange(s_p))
    tab_d = _rope_tables(jnp.full((d_b,), past_len))
    pool_kpe_t = jnp.swapaxes(cache_kpe, 2, 3)

    wg = w_ffn_gate.astype(BF16)
    wu = w_ffn_up.astype(BF16)
    wd = w_ffn_down.astype(BF16)

    xp = x_prompt.reshape(n_p, d)
    xd = x_sample.reshape(d_b, d)
    outs = {k: [] for k in ("ckv_p", "kpe_p", "ckv_d", "kpe_d", "hg_p", "hg_d", "cv_p", "cv_d")}

    for layer in range(depth):
        j = layer // 2
        last = layer == depth - 1
        xp = _ffn(xp, norm_g[layer, 0], wg[layer, 0], wu[layer, 0], wd[layer, 0])
        xd = _ffn(xd, norm_g[layer, 0], wg[layer, 0], wu[layer, 0], wd[layer, 0])
        ffn_b = (norm_g[layer, 2], wg[layer, 1], wu[layer, 1], wd[layer, 1])
        fin = final_norm_g if last else None
        if layer % 2 == 0:
            w = _prep_ab_weights(w_in_ab[j], g_q_lora[j], g_kv_lora[j], w_uq[j], w_ukv[j], w_out_ab[j])
            ckv, kpe_t, q, k, v, hq, hf, hi, hg = _ab_pre(xp, norm_g[layer, 1], w, tab_p, hgrn_lower_bound, layer)
            o_a = _attn_prompt(q.reshape(b_p, s_p, -1), k.reshape(b_p, s_p, -1), v.reshape(b_p, s_p, -1))
            sh = (b_p, s_p, -1)
            o_b, st_p = _hgrn_prompt(hq.reshape(sh), hf.reshape(sh), hi.reshape(sh), hg.reshape(sh), g_hgrn_out[j])
            pre = ((o_a.reshape(n_p, -1), w["w_out_a"]), (o_b.reshape(n_p, -1), w["w_out_b"]))
            xp = _ffn(xp, *ffn_b, pre=pre, final_g=fin)
            outs["ckv_p"].append(ckv.reshape(b_p, s_p, -1))
            outs["kpe_p"].append(jnp.swapaxes(kpe_t, 1, 2))
            outs["hg_p"].append(st_p)
            ckv, kpe_t, q, _, _, hq, hf, hi, hg = _ab_pre(xd, norm_g[layer, 1], w, tab_d, hgrn_lower_bound, layer)
            kpe = kpe_t[0].T
            qd = _small_matmul(q, w["w_dec_q"], F32)
            n_lat = A_HEADS * A_KV_LORA
            o_lat = _attn_decode(page_table, qd[:, :n_lat].reshape(d_b, A_HEADS, A_KV_LORA),
                                 qd[:, n_lat:].reshape(d_b, A_HEADS, A_ROPE), ckv.reshape(d_b, 1, -1),
                                 kpe.reshape(d_b, 1, -1), cache_ckv, pool_kpe_t, j)
            o_a = _small_matmul(o_lat.reshape(d_b, n_lat), w["w_dec_o"], BF16)
            o_b, st_d = _hgrn_step(hq, hf, hi, hg, g_hgrn_out[j], state_hgrn, j)
            xd = _ffn(xd, *ffn_b, pre=((o_a, w["w_out_a"]), (o_b, w["w_out_b"])), final_g=fin)
            outs["ckv_d"].append(ckv)
            outs["kpe_d"].append(kpe)
            outs["hg_d"].append(st_d)
        else:
            win = w_in_c[j].astype(BF16)
            wout = w_out_c[j].astype(BF16)
            xp3, buf_p = _cmix_prompt(xp.reshape(b_p, s_p, d), norm_g[layer, 1], win, w_conv_c[j], wout)
            xp = _ffn(xp3.reshape(n_p, d), *ffn_b, final_g=fin)
            xd, v_new = _cmix_step(xd, state_conv[j, :, 0], state_conv[j, :, 1], norm_g[layer, 1], win, w_conv_c[j], wout)
            xd = _ffn(xd, *ffn_b, final_g=fin)
            outs["cv_p"].append(buf_p)
            outs["cv_d"].append(jnp.stack([state_conv[j, :, 1], v_new], axis=1))

    return (xp.reshape(b_p, s_p, d), xd.reshape(d_b, s_d, d),
            jnp.stack(outs["ckv_p"]), jnp.stack(outs["kpe_p"]),
            jnp.stack(outs["ckv_d"]).reshape(-1, d_b, s_d, A_KV_LORA), jnp.stack(outs["kpe_d"]).reshape(-1, d_b, s_d, A_ROPE),
            jnp.stack(outs["hg_p"]), jnp.stack(outs["hg_d"]), jnp.stack(outs["cv_p"]), jnp.stack(outs["cv_d"]))
```

```python
import functools

import jax
import jax.numpy as jnp
from jax import lax
from jax.experimental import pallas as pl
from jax.experimental.pallas import tpu as pltpu

F32 = jnp.float32
BF16 = jnp.bfloat16

A_HEADS = 8
A_NOPE = 64
A_ROPE = 32
A_V = 64
A_Q_LORA = 384
A_KV_LORA = 256
ROPE_BASE = 10000.0
SM_SCALE = (A_NOPE + A_ROPE) ** -0.5
LOG2_E = 1.4426950408889634
B_HEADS = 4
B_DK = 128
B_DV = 128
B_CHUNK = 16
C_KERNEL = 3
RMS_EPS = 1e-6

LANES = 128
HEAD_SLOT = LANES
HGRN_BLOCK = 128
HGRN_ROWS_PER_STEP = 512
VMEM_LIMIT = 56 * 1024 * 1024
DECODE_BUFFER_BYTES = 40 * 1024 * 1024


def _params(sem, vmem=VMEM_LIMIT):
    return pltpu.CompilerParams(dimension_semantics=sem, vmem_limit_bytes=vmem)


def _const_spec(shape):
    nd = len(shape)
    return pl.BlockSpec(shape, lambda *_: (0,) * nd, pipeline_mode=pl.Buffered(1))


def _rms(x, g):
    return x * lax.rsqrt(jnp.mean(x * x, axis=-1, keepdims=True) + RMS_EPS) * g


def _dot(a, b):
    return jnp.dot(a, b, preferred_element_type=F32)


def _dot_nt(a, b):
    return lax.dot_general(a, b, (((1,), (1,)), ((), ())), preferred_element_type=F32)


def _dot_tn(a, b):
    return lax.dot_general(a, b, (((0,), (0,)), ((), ())), preferred_element_type=F32)


def _silu(x):
    return x * jax.nn.sigmoid(x)


def _row_tile(n, want):
    t = min(n, want)
    assert n % t == 0, (n, t)
    return t


def _ffn_body(*refs, n_chunks, tf, n_pre, has_final):
    x_ref = refs[0]
    pre = refs[1:1 + 2 * n_pre]
    g_ref, wg_ref, wu_ref, wd_ref = refs[1 + 2 * n_pre:5 + 2 * n_pre]
    rest = refs[5 + 2 * n_pre:]
    gf_ref = rest[0] if has_final else None
    o_ref = rest[-1]

    x = x_ref[...]
    for p in range(n_pre):
        x = x + _dot(pre[2 * p][...], pre[2 * p + 1][...])
    h = _rms(x, g_ref[...]).astype(BF16)
    acc = None
    for c in range(n_chunks):
        cs = slice(c * tf, (c + 1) * tf)
        gate = _dot(h, wg_ref[:, cs])
        up = _dot(h, wu_ref[:, cs])
        a = (_silu(gate) * up).astype(BF16)
        part = _dot(a, wd_ref[cs, :])
        acc = part if acc is None else acc + part
    y = x + 0.5 * acc
    if has_final:
        y = _rms(y, gf_ref[...])
    o_ref[...] = y


def _ffn(x, g, weights, which, pre=(), final_g=None, tm=1024):
    n, d = x.shape
    f = weights[0].shape[-1]
    tf = 256 if f % 256 == 0 else LANES
    tm = _row_tile(n, tm)
    row = lambda i: (i, 0)
    in_specs = [pl.BlockSpec((tm, d), row)]
    args = [x]
    for act, w in pre:
        in_specs += [pl.BlockSpec((tm, act.shape[1]), row), _const_spec(w.shape)]
        args += [act, w]
    pick = lambda i: (*which, 0, 0)
    w_specs = [pl.BlockSpec((None, None) + w.shape[2:], pick, pipeline_mode=pl.Buffered(1)) for w in weights]
    in_specs += [_const_spec((1, d))] + w_specs
    args += [g.reshape(1, d), *weights]
    if final_g is not None:
        in_specs.append(_const_spec((1, d)))
        args.append(final_g.reshape(1, d))
    body = functools.partial(_ffn_body, n_chunks=f // tf, tf=tf, n_pre=len(pre), has_final=final_g is not None)
    return pl.pallas_call(
        body,
        grid=(n // tm,),
        in_specs=in_specs,
        out_specs=pl.BlockSpec((tm, d), row),
        out_shape=jax.ShapeDtypeStruct((n, d), F32),
        compiler_params=_params(("parallel",)),
        name="half_ffn",
    )(*args)


def _rope_slot(x, cos, sin_lo, sin_hi):
    half = A_ROPE // 2
    return x * cos + pltpu.roll(x, HEAD_SLOT - half, 1) * sin_lo + pltpu.roll(x, half, 1) * sin_hi


def _ab_pre_body(x_ref, g_ref, win_ref, gq_ref, gkv_ref, wq_ref, wkv_ref, cq_ref, sq_lo_ref, sq_hi_ref,
                 c_ref, s_lo_ref, s_hi_ref, lb_ref,
                 ckv_ref, kpet_ref, q_ref, k_ref, v_ref, hq_ref, hf_ref, hi_ref, hg_ref, *, layer):
    hb = B_HEADS * B_DK
    h = _rms(x_ref[...], g_ref[...]).astype(BF16)
    proj = _dot(h, win_ref[...])
    o_kv = A_Q_LORA
    o_pe = o_kv + A_KV_LORA
    o_h = o_pe + HEAD_SLOT
    q_tabs = (cq_ref[...], sq_lo_ref[...], sq_hi_ref[...])

    cq = _rms(proj[:, :A_Q_LORA], gq_ref[...]).astype(BF16)
    qq = _dot(cq, wq_ref[...])
    hw = A_HEADS * HEAD_SLOT
    for hd in range(A_HEADS):
        sl = slice(hd * HEAD_SLOT, (hd + 1) * HEAD_SLOT)
        q_ref[:, sl] = _rope_slot(qq[:, sl], *q_tabs).astype(BF16)

    ckv = _rms(proj[:, o_kv:o_pe], gkv_ref[...])
    ckv_ref[...] = ckv
    kpe = _rope_slot(proj[:, o_pe:o_h], c_ref[...], s_lo_ref[...], s_hi_ref[...])
    kpet_ref[0] = kpe.T[A_NOPE:A_NOPE + A_ROPE, :]
    kv = _dot(ckv.astype(BF16), wkv_ref[...])
    for hd in range(A_HEADS):
        sl = slice(hd * HEAD_SLOT, (hd + 1) * HEAD_SLOT)
        k_ref[:, sl] = (kv[:, sl] + kpe).astype(BF16)
    v_ref[...] = kv[:, hw:].astype(BF16)

    lbr = lb_ref[...]
    e = jnp.exp(lbr - jnp.max(lbr, axis=0, keepdims=True))
    lb = jnp.sum(e[:layer + 1], axis=0, keepdims=True) / jnp.sum(e, axis=0, keepdims=True)
    hq_ref[...] = _silu(proj[:, o_h:o_h + hb]) * (B_DK ** -0.5)
    hf_ref[...] = lb + (1.0 - lb) * jax.nn.sigmoid(proj[:, o_h + hb:o_h + 2 * hb])
    hi_ref[...] = proj[:, o_h + 2 * hb:o_h + 3 * hb].astype(BF16)
    hg_ref[...] = _silu(proj[:, o_h + 3 * hb:o_h + 4 * hb])


def _ab_pre(x, g, w, tables, lb_raw, layer, tm=1024):
    n, d = x.shape
    s_len = tables[0].shape[0]
    tm = _row_tile(s_len, tm)
    assert n % s_len == 0
    n_pos_tiles = s_len // tm
    row = lambda i: (i, 0)
    pos = lambda i: (i % n_pos_tiles, 0)
    hw = A_HEADS * HEAD_SLOT
    hb = B_HEADS * B_DK
    outs = [
        (A_KV_LORA, F32), None, (hw, BF16), (hw, BF16), (A_HEADS * A_V, BF16),
        (hb, F32), (hb, F32), (hb, BF16), (hb, F32),
    ]
    kpet_spec = pl.BlockSpec((1, A_ROPE, tm), lambda i: (i // n_pos_tiles, 0, i % n_pos_tiles))
    kpet_shape = jax.ShapeDtypeStruct((n // s_len, A_ROPE, s_len), F32)
    return pl.pallas_call(
        functools.partial(_ab_pre_body, layer=layer),
        grid=(n // tm,),
        in_specs=[
            pl.BlockSpec((tm, d), row), _const_spec((1, d)), _const_spec(w["w_in"].shape),
            _const_spec((1, A_Q_LORA)), _const_spec((1, A_KV_LORA)), _const_spec(w["w_q"].shape),
            _const_spec(w["w_kv"].shape)] + [pl.BlockSpec((tm, HEAD_SLOT), pos)] * len(tables) + [_const_spec(lb_raw.shape)],
        out_specs=[kpet_spec if o is None else pl.BlockSpec((tm, o[0]), row) for o in outs],
        out_shape=[kpet_shape if o is None else jax.ShapeDtypeStruct((n, o[0]), o[1]) for o in outs],
        compiler_params=_params(("parallel",)),
        name="ab_input_stage",
    )(x, g.reshape(1, d), w["w_in"], w["g_q"], w["g_kv"], w["w_q"], w["w_kv"], *tables, lb_raw)


def _attn_prompt_body(q_ref, k_ref, v_ref, o_ref, vaug, *, tq):
    s_len = q_ref.shape[1]
    nq = s_len // tq
    nv = 2 * A_V
    lane = lax.broadcasted_iota(jnp.int32, (tq, nv), 1)
    r_i = lax.broadcasted_iota(jnp.int32, (tq, tq), 0)
    c_i = lax.broadcasted_iota(jnp.int32, (tq, tq), 1)
    causal = c_i <= r_i
    vaug[:, :nv] = v_ref[0]
    vaug[:, nv:] = jnp.ones((s_len, nv), BF16)
    for qi in range(nq):
        rows = slice(qi * tq, (qi + 1) * tq)
        n_past = qi * tq
        outs = []
        for hh in range(2):
            hs = slice(hh * HEAD_SLOT, (hh + 1) * HEAD_SLOT)
            q = q_ref[0, rows, hs]
            s_d = jnp.where(causal, _dot_nt(q, k_ref[0, rows, hs]), -jnp.inf)
            m = jnp.max(s_d, axis=-1, keepdims=True)
            if n_past:
                s_p = _dot_nt(q, k_ref[0, :n_past, hs])
                m = jnp.maximum(m, jnp.max(s_p, axis=-1, keepdims=True))
            o = _dot(jnp.exp2(s_d - m).astype(BF16), vaug[rows, :])
            if n_past:
                o = o + _dot(jnp.exp2(s_p - m).astype(BF16), vaug[:n_past, :])
            outs.append(o[:, :nv] / o[:, nv:])
        o_ref[0, rows, :] = jnp.where(lane < A_V, outs[0], outs[1]).astype(BF16)


def _attn_prompt(q, k, v, tq=256):
    b, s, _ = q.shape
    tq = _row_tile(s, tq)
    blk2 = lambda i, j: (i, 0, j)
    return pl.pallas_call(
        functools.partial(_attn_prompt_body, tq=tq),
        grid=(b, A_HEADS // 2),
        in_specs=[
            pl.BlockSpec((1, s, 2 * HEAD_SLOT), blk2), pl.BlockSpec((1, s, 2 * HEAD_SLOT), blk2),
            pl.BlockSpec((1, s, 2 * A_V), blk2),
        ],
        out_specs=pl.BlockSpec((1, s, 2 * A_V), blk2),
        out_shape=jax.ShapeDtypeStruct((b, s, A_HEADS * A_V), BF16),
        scratch_shapes=[pltpu.VMEM((s, 4 * A_V), BF16)],
        compiler_params=_params(("parallel", "parallel")),
        name="mla_prompt_attention",
    )(q, k, v)


def _hgrn_prompt_body(*refs, n_sb):
    for phase in _hgrn_phases(*refs, n_sb=n_sb):
        phase()


def _hgrn_phases(q_ref, f_ref, i_ref, gt_ref, go_ref, o_ref, st_ref, st_scr, *, n_sb):
    t = pl.program_id(1)

    def prologue():
        @pl.when(t == 0)
        def _():
            st_scr[...] = jnp.zeros_like(st_scr)

    def epilogue():
        @pl.when(t == pl.num_programs(1) - 1)
        def _():
            for hd in range(B_HEADS):
                st_ref[0, hd] = st_scr[hd].T

    main = functools.partial(_hgrn_main, q_ref, f_ref, i_ref, gt_ref, go_ref, o_ref, st_scr, n_sb=n_sb)
    return prologue, main, epilogue


def _hgrn_main(q_ref, f_ref, i_ref, gt_ref, go_ref, o_ref, st_scr, *, n_sb):
    R = HGRN_BLOCK
    hb = B_HEADS * B_DK
    r_i = lax.broadcasted_iota(jnp.int32, (R, R), 0)
    c_i = lax.broadcasted_iota(jnp.int32, (R, R), 1)
    tri = jnp.where(c_i <= r_i, 1.0, 0.0).astype(BF16)
    base_mask = ((r_i >> 4) == (c_i >> 4)) & (c_i <= r_i)
    levels = []
    m = B_CHUNK
    while 2 * m <= R:
        sh = (2 * m).bit_length() - 1
        levels.append((m, ((r_i >> sh) == (c_i >> sh)) & ((r_i & (2 * m - 1)) >= m) & ((c_i & (2 * m - 1)) < m)))
        m *= 2
    go = go_ref[...]

    for sb in range(n_sb):
        rows = slice(sb * R, (sb + 1) * R)
        q = q_ref[0, rows, :]
        f = f_ref[0, rows, :]
        v = i_ref[0, rows, :]
        k = 1.0 - f
        lf = jnp.log(f)
        hi = lf.astype(BF16)
        r1 = lf - hi.astype(F32)
        mid = r1.astype(BF16)
        lo = (r1 - mid.astype(F32)).astype(BF16)
        G = _dot(tri, hi) + _dot(tri, mid) + _dot(tri, lo)
        g_last = G[R - 1:R, :]

        n_ch = R // B_CHUNK
        end_rows = [G[B_CHUNK * (j + 1) - 1:B_CHUNK * (j + 1), :] for j in range(n_ch)]
        g_end = jnp.concatenate(end_rows, axis=0)
        g_start = jnp.concatenate([jnp.zeros((1, hb), F32)] + end_rows[:-1], axis=0)

        def expand(per_chunk):
            return jnp.concatenate([jnp.broadcast_to(per_chunk[j:j + 1, :], (B_CHUNK, hb)) for j in range(n_ch)], axis=0)

        g0 = expand(g_start)
        q0 = q * jnp.exp(G - g0)
        k_end = k * jnp.exp(expand(g_end) - G)
        q_lv = [q0.astype(BF16)]
        k_lv = [(k * jnp.exp(g0 - G)).astype(BF16)]
        q_int = (q0 * expand(jnp.exp(g_start))).astype(BF16)
        k_st = (k_end * expand(jnp.exp(g_last - g_end))).astype(BF16)
        for mm, _ in levels:
            cpb = 2 * mm // B_CHUNK
            g_mid = jnp.concatenate([end_rows[(j // cpb) * cpb + cpb // 2 - 1] for j in range(n_ch)], axis=0)
            q_lv.append((q0 * expand(jnp.exp(jnp.minimum(g_start - g_mid, 0.0)))).astype(BF16))
            k_lv.append((k_end * expand(jnp.exp(jnp.minimum(g_mid - g_end, 0.0)))).astype(BF16))
        decay = jnp.exp(g_last)

        for hd in range(B_HEADS):
            hs = slice(hd * B_DK, (hd + 1) * B_DK)
            att = jnp.where(base_mask, _dot_nt(q_lv[0][:, hs], k_lv[0][:, hs]), 0.0)
            for li, (_, msk) in enumerate(levels):
                att = jnp.where(msk, _dot_nt(q_lv[li + 1][:, hs], k_lv[li + 1][:, hs]), att)
            st = st_scr[hd]
            o = _dot(att.astype(BF16), v[:, hs]) + _dot_nt(q_int[:, hs], st.astype(BF16))
            st_scr[hd] = st * decay[:, hs] + _dot_tn(v[:, hs], k_st[:, hs])
            o = o * lax.rsqrt(jnp.mean(o * o, axis=-1, keepdims=True) + RMS_EPS) * go
            o_ref[0, rows, hs] = (o * gt_ref[0, rows, hs]).astype(BF16)


def _hgrn_prompt(hq, hf, hi, hg, g_out, rows_per_step):
    b, s, hb = hq.shape
    assert s % HGRN_BLOCK == 0
    ts = rows_per_step
    blk = lambda i, t: (i, t, 0)
    return pl.pallas_call(
        functools.partial(_hgrn_prompt_body, n_sb=ts // HGRN_BLOCK),
        grid=(b, s // ts),
        in_specs=[pl.BlockSpec((1, ts, hb), blk)] * 4 + [pl.BlockSpec((1, B_DV), lambda i, t: (0, 0))],
        out_specs=[
            pl.BlockSpec((1, ts, hb), blk),
            pl.BlockSpec((1, B_HEADS, B_DK, B_DV), lambda i, t: (i, 0, 0, 0)),
        ],
        out_shape=[
            jax.ShapeDtypeStruct((b, s, hb), BF16),
            jax.ShapeDtypeStruct((b, B_HEADS, B_DK, B_DV), F32),
        ],
        scratch_shapes=[pltpu.VMEM((B_HEADS, B_DV, B_DK), F32)],
        compiler_params=_params(("parallel", "arbitrary")),
        name="hgrn2_prompt",
    )(hq, hf, hi, hg, g_out.reshape(1, B_DV))


def _hgrn_step_body(q_ref, f_ref, i_ref, gt_ref, go_ref, s_ref, o_ref, so_ref, *, nb):
    go = go_ref[...]
    pad = jnp.zeros((B_DK - 3 * nb, B_DK), F32)
    for hd in range(B_HEADS):
        hs = slice(hd * B_DK, (hd + 1) * B_DK)
        f = f_ref[:, hs]
        cols = jnp.concatenate([q_ref[:, hs], f, 1.0 - f, pad], axis=0).T
        for j in range(nb):
            s_new = cols[:, nb + j:nb + j + 1] * s_ref[j, hd] + cols[:, 2 * nb + j:2 * nb + j + 1] * i_ref[j:j + 1, hs].astype(F32)
            so_ref[j, hd] = s_new
            o = jnp.sum(cols[:, j:j + 1] * s_new, axis=0, keepdims=True)
            o = o * lax.rsqrt(jnp.mean(o * o, axis=-1, keepdims=True) + RMS_EPS) * go
            o_ref[j:j + 1, hs] = (o * gt_ref[j:j + 1, hs]).astype(BF16)


def _hgrn_step(hq, hf, hi, hg, g_out, state, layer, nb=8):
    n, hb = hq.shape
    nb = _row_tile(n, nb)
    row = lambda i: (i, 0)
    st_in = lambda i: (layer, i, 0, 0, 0)
    st_out = lambda i: (i, 0, 0, 0)
    return pl.pallas_call(
        functools.partial(_hgrn_step_body, nb=nb),
        grid=(n // nb,),
        in_specs=[pl.BlockSpec((nb, hb), row)] * 4 + [_const_spec((1, B_DV)), pl.BlockSpec((None, nb, B_HEADS, B_DK, B_DV), st_in)],
        out_specs=[pl.BlockSpec((nb, hb), row), pl.BlockSpec((nb, B_HEADS, B_DK, B_DV), st_out)],
        out_shape=[jax.ShapeDtypeStruct((n, hb), BF16), jax.ShapeDtypeStruct(state.shape[1:], F32)],
        compiler_params=_params(("parallel",)),
        name="hgrn2_step",
    )(hq, hf, hi, hg, g_out.reshape(1, B_DV), state)


def _small_matmul_body(a_ref, w_ref, o_ref):
    o_ref[...] = _dot(a_ref[...].astype(BF16), w_ref[...]).astype(o_ref.dtype)


def _small_matmul(a, w, out_dtype):
    n = a.shape[0]
    out_spec = pl.BlockSpec((n, w.shape[1]), lambda i: (0, 0))
    return pl.pallas_call(
        _small_matmul_body,
        grid=(1,),
        in_specs=[_const_spec(a.shape), _const_spec(w.shape)],
        out_specs=out_spec,
        out_shape=jax.ShapeDtypeStruct((n, w.shape[1]), out_dtype),
        compiler_params=_params(("arbitrary",)),
        name="sample_matmul",
    )(a, w)


def _page_copies(pt_ref, ckv_hbm, kpe_hbm, ckv_buf, kpe_buf, sems, step, slot, *, n_chunk, pc, page):
    b = step // n_chunk
    c = step % n_chunk
    copies = []
    for p in range(pc):
        pg = pt_ref[b, c * pc + p]
        copies.append(pltpu.make_async_copy(ckv_hbm.at[pg], ckv_buf.at[slot, pl.ds(p * page, page), :], sems.at[slot, 0]))
        copies.append(pltpu.make_async_copy(kpe_hbm.at[pg], kpe_buf.at[slot, :, pl.ds(p * page, page)], sems.at[slot, 1]))
    return copies


def _attn_decode_body(pt_ref, *refs, **kw):
    for phase in _decode_phases(pl.program_id(0), pl.num_programs(0), pt_ref, *refs, **kw):
        phase()


def _decode_phases(s, total, pt_ref, ql_ref, qp_ref, cn_ref, kn_ref, ckv_hbm, kpe_hbm, o_ref,
                   ckv_buf, kpe_buf, sems, m_scr, l_scr, acc_scr, *, layer, n_chunk, pc, page):
    slot = s % 2
    c = s % n_chunk
    copies = functools.partial(_page_copies, pt_ref, ckv_hbm.at[layer], kpe_hbm.at[layer], ckv_buf, kpe_buf, sems,
                               n_chunk=n_chunk, pc=pc, page=page)

    def prologue():
        @pl.when(s == 0)
        def _():
            for cp in copies(s, slot):
                cp.start()

        @pl.when(s + 1 < total)
        def _():
            for cp in copies(s + 1, 1 - slot):
                cp.start()

        for cp in copies(s, slot):
            cp.wait()

        @pl.when(c == 0)
        def _():
            cn = cn_ref[0]
            s_self = (jnp.sum(ql_ref[0] * cn, axis=-1, keepdims=True)
                      + jnp.sum(qp_ref[0] * kn_ref[0], axis=-1, keepdims=True))
            m_scr[...] = s_self
            l_scr[...] = jnp.ones_like(l_scr)
            acc_scr[...] = jnp.broadcast_to(cn, acc_scr.shape)

    def main():
        ql = ql_ref[0]
        qp = qp_ref[0]
        ckv = ckv_buf[slot]
        sc = _dot_nt(ql, ckv) + _dot(qp, kpe_buf[slot])
        m_old = m_scr[...]
        m_new = jnp.maximum(m_old, jnp.max(sc, axis=-1, keepdims=True))
        alpha = jnp.exp2(m_old - m_new)
        p = jnp.exp2(sc - m_new)
        l_scr[...] = alpha * l_scr[...] + jnp.sum(p, axis=-1, keepdims=True)
        acc_scr[...] = alpha * acc_scr[...] + _dot(p, ckv)
        m_scr[...] = m_new

    def epilogue():
        @pl.when(c == n_chunk - 1)
        def _():
            o_ref[0] = acc_scr[...] / l_scr[...]

    return prologue, main, epilogue


def _decode_pages_per_step(n_pages, page):
    budget = DECODE_BUFFER_BYTES // (2 * page * (A_KV_LORA + A_ROPE) * 4)
    return max(p for p in range(1, n_pages + 1) if n_pages % p == 0 and p <= max(budget, 1))


def _attn_decode(page_table, q_lat, q_pe, ckv_new, kpe_new, pool_ckv, pool_kpe_t, layer):
    db, n_pages = page_table.shape
    page = pool_ckv.shape[2]
    pc = _decode_pages_per_step(n_pages, page)
    n_chunk = n_pages // pc
    per_row = lambda s, pt: (s // n_chunk, 0, 0)
    grid_spec = pltpu.PrefetchScalarGridSpec(
        num_scalar_prefetch=1,
        grid=(db * n_chunk,),
        in_specs=[
            pl.BlockSpec((1, A_HEADS, A_KV_LORA), per_row), pl.BlockSpec((1, A_HEADS, A_ROPE), per_row),
            pl.BlockSpec((1, 1, A_KV_LORA), per_row), pl.BlockSpec((1, 1, A_ROPE), per_row),
            pl.BlockSpec(memory_space=pl.ANY), pl.BlockSpec(memory_space=pl.ANY),
        ],
        out_specs=pl.BlockSpec((1, A_HEADS, A_KV_LORA), per_row),
        scratch_shapes=_decode_scratch(pc, page),
    )
    return pl.pallas_call(
        functools.partial(_attn_decode_body, layer=layer, n_chunk=n_chunk, pc=pc, page=page),
        grid_spec=grid_spec,
        out_shape=jax.ShapeDtypeStruct((db, A_HEADS, A_KV_LORA), F32),
        compiler_params=_params(("arbitrary",)),
        name="mla_paged_decode",
    )(page_table, q_lat, q_pe, ckv_new, kpe_new, pool_ckv, pool_kpe_t)


def _decode_scratch(pc, page):
    return [
        pltpu.VMEM((2, pc * page, A_KV_LORA), F32), pltpu.VMEM((2, A_ROPE, pc * page), F32),
        pltpu.SemaphoreType.DMA((2, 2)),
        pltpu.VMEM((A_HEADS, 1), F32), pltpu.VMEM((A_HEADS, 1), F32), pltpu.VMEM((A_HEADS, A_KV_LORA), F32),
    ]


def _hgrn_decode_body(pt_ref, q_ref, f_ref, i_ref, gt_ref, go_ref, ql_ref, qp_ref, cn_ref, kn_ref, ckv_hbm, kpe_hbm,
                      o_ref, st_ref, ol_ref, st_scr, ckv_buf, kpe_buf, sems, m_scr, l_scr, acc_scr, *, n_sb, decode_kw):
    step = pl.program_id(0) * pl.num_programs(1) + pl.program_id(1)
    total = pl.num_programs(0) * pl.num_programs(1)
    dec = _decode_phases(step, total, pt_ref, ql_ref, qp_ref, cn_ref, kn_ref, ckv_hbm, kpe_hbm, ol_ref,
                         ckv_buf, kpe_buf, sems, m_scr, l_scr, acc_scr, **decode_kw)
    hgr = _hgrn_phases(q_ref, f_ref, i_ref, gt_ref, go_ref, o_ref, st_ref, st_scr, n_sb=n_sb)
    for dec_phase, hgr_phase in zip(dec, hgr):
        dec_phase()
        hgr_phase()


def _merged_chunks(n_steps, db, n_pages, page):
    if n_steps % db:
        return None
    n_chunk = n_steps // db
    if n_pages % n_chunk or n_pages // n_chunk > _decode_pages_per_step(n_pages, page):
        return None
    return n_chunk


def _hgrn_prompt_with_decode(hq, hf, hi, hg, g_out, page_table, q_lat, q_pe, ckv_new, kpe_new, pool_ckv, pool_kpe_t,
                             layer, n_chunk, rows_per_step):
    b, s, hb = hq.shape
    ts = rows_per_step
    nt = s // ts
    db, n_pages = page_table.shape
    page = pool_ckv.shape[2]
    pc = n_pages // n_chunk
    blk = lambda i, t, pt: (i, t, 0)
    per_row = lambda i, t, pt: ((i * nt + t) // n_chunk, 0, 0)
    grid_spec = pltpu.PrefetchScalarGridSpec(
        num_scalar_prefetch=1,
        grid=(b, nt),
        in_specs=[pl.BlockSpec((1, ts, hb), blk)] * 4 + [
            pl.BlockSpec((1, B_DV), lambda i, t, pt: (0, 0)),
            pl.BlockSpec((1, A_HEADS, A_KV_LORA), per_row), pl.BlockSpec((1, A_HEADS, A_ROPE), per_row),
            pl.BlockSpec((1, 1, A_KV_LORA), per_row), pl.BlockSpec((1, 1, A_ROPE), per_row),
            pl.BlockSpec(memory_space=pl.ANY), pl.BlockSpec(memory_space=pl.ANY),
        ],
        out_specs=[
            pl.BlockSpec((1, ts, hb), blk),
            pl.BlockSpec((1, B_HEADS, B_DK, B_DV), lambda i, t, pt: (i, 0, 0, 0)),
            pl.BlockSpec((1, A_HEADS, A_KV_LORA), per_row),
        ],
        scratch_shapes=[pltpu.VMEM((B_HEADS, B_DV, B_DK), F32)] + _decode_scratch(pc, page),
    )
    decode_kw = dict(layer=layer, n_chunk=n_chunk, pc=pc, page=page)
    return pl.pallas_call(
        functools.partial(_hgrn_decode_body, n_sb=ts // HGRN_BLOCK, decode_kw=decode_kw),
        grid_spec=grid_spec,
        out_shape=[
            jax.ShapeDtypeStruct((b, s, hb), BF16),
            jax.ShapeDtypeStruct((b, B_HEADS, B_DK, B_DV), F32),
            jax.ShapeDtypeStruct((db, A_HEADS, A_KV_LORA), F32),
        ],
        compiler_params=_params(("arbitrary", "arbitrary")),
        name="hgrn2_prompt_with_decode",
    )(page_table, hq, hf, hi, hg, g_out.reshape(1, B_DV), q_lat, q_pe, ckv_new, kpe_new, pool_ckv, pool_kpe_t)


def _cmix_prompt_body(x_ref, g_ref, win_ref, wc_ref, wout_ref, o_ref, buf_ref, carry, *, n_tiles):
    t = pl.program_id(1)
    tm = x_ref.shape[1]
    cw = wout_ref.shape[0]

    @pl.when(t == 0)
    def _():
        carry[...] = jnp.zeros_like(carry)

    x = x_ref[0]
    h = _rms(x, g_ref[...]).astype(BF16)
    p = _dot(h, win_ref[...])
    v = p[:, cw:2 * cw] * p[:, 2 * cw:]
    row = lax.broadcasted_iota(jnp.int32, v.shape, 0)
    prev1 = jnp.broadcast_to(carry[1:2, :], v.shape)
    prev2 = jnp.broadcast_to(carry[0:1, :], v.shape)
    v1 = jnp.where(row == 0, prev1, pltpu.roll(v, 1, 0))
    v2 = jnp.where(row == 0, prev2, jnp.where(row == 1, prev1, pltpu.roll(v, 2, 0)))
    y = wc_ref[0:1, :] * v2 + wc_ref[1:2, :] * v1 + wc_ref[2:3, :] * v
    o_ref[0] = x + _dot((p[:, :cw] * y).astype(BF16), wout_ref[...])
    carry[0:2, :] = v[tm - 2:, :]

    @pl.when(t == n_tiles - 1)
    def _():
        buf_ref[0] = v[tm - 2:, :]


def _cmix_prompt(x, g, w_in, w_conv, w_out, tm=1024):
    b, s, d = x.shape
    tm = _row_tile(s, tm)
    n_tiles = s // tm
    cw = w_out.shape[0]
    blk = lambda i, t: (i, t, 0)
    cst = lambda shape: pl.BlockSpec(shape, lambda i, t: (0,) * len(shape), pipeline_mode=pl.Buffered(1))
    return pl.pallas_call(
        functools.partial(_cmix_prompt_body, n_tiles=n_tiles),
        grid=(b, n_tiles),
        in_specs=[pl.BlockSpec((1, tm, d), blk), cst((1, d)), cst(w_in.shape), cst(w_conv.shape), cst(w_out.shape)],
        out_specs=[pl.BlockSpec((1, tm, d), blk), pl.BlockSpec((1, C_KERNEL - 1, cw), lambda i, t: (i, 0, 0))],
        out_shape=[jax.ShapeDtypeStruct((b, s, d), F32), jax.ShapeDtypeStruct((b, C_KERNEL - 1, cw), F32)],
        scratch_shapes=[pltpu.VMEM((8, cw), F32)],
        compiler_params=_params(("parallel", "arbitrary")),
        name="conv_mixer_prompt",
    )(x, g.reshape(1, d), w_in, w_conv, w_out)


def _cmix_step_body(x_ref, b0_ref, b1_ref, g_ref, win_ref, wc_ref, wout_ref, o_ref, v_ref):
    cw = wout_ref.shape[0]
    x = x_ref[...]
    h = _rms(x, g_ref[...]).astype(BF16)
    p = _dot(h, win_ref[...])
    v = p[:, cw:2 * cw] * p[:, 2 * cw:]
    y = wc_ref[0:1, :] * b0_ref[...] + wc_ref[1:2, :] * b1_ref[...] + wc_ref[2:3, :] * v
    o_ref[...] = x + _dot((p[:, :cw] * y).astype(BF16), wout_ref[...])
    v_ref[...] = v


def _cmix_step(x, buf0, buf1, g, w_in, w_conv, w_out):
    n, d = x.shape
    cw = w_out.shape[0]
    args = (x, buf0, buf1, g.reshape(1, d), w_in, w_conv, w_out)
    return pl.pallas_call(
        _cmix_step_body,
        grid=(1,),
        in_specs=[_const_spec(a.shape) for a in args],
        out_specs=[pl.BlockSpec((n, d), lambda i: (0, 0)), pl.BlockSpec((n, cw), lambda i: (0, 0))],
        out_shape=[jax.ShapeDtypeStruct((n, d), F32), jax.ShapeDtypeStruct((n, cw), F32)],
        compiler_params=_params(("arbitrary",)),
        name="conv_mixer_step",
    )(*args)


def _prep_ab_weights(w_in, g_q, g_kv, w_uq, w_ukv, w_out):
    d = w_in.shape[0]
    o_kv = A_Q_LORA
    o_pe = o_kv + A_KV_LORA
    o_h = o_pe + A_ROPE
    zn = jnp.zeros((d, A_NOPE), F32)
    zt = jnp.zeros((d, HEAD_SLOT - A_NOPE - A_ROPE), F32)
    w_in_p = jnp.concatenate([w_in[:, :o_pe], zn, w_in[:, o_pe:o_h], zt, w_in[:, o_h:]], axis=1).astype(BF16)

    wq = w_uq.reshape(A_Q_LORA, A_HEADS, A_NOPE + A_ROPE)
    zq = jnp.zeros((A_Q_LORA, A_HEADS, HEAD_SLOT - A_NOPE - A_ROPE), F32)
    w_q = jnp.concatenate([wq, zq], axis=-1).reshape(A_Q_LORA, A_HEADS * HEAD_SLOT).astype(BF16)

    wkv = w_ukv.reshape(A_KV_LORA, A_HEADS, A_NOPE + A_V)
    w_uk, w_uv = wkv[..., :A_NOPE], wkv[..., A_NOPE:]
    k_pad = jnp.concatenate([w_uk, jnp.zeros((A_KV_LORA, A_HEADS, HEAD_SLOT - A_NOPE), F32)], axis=-1)
    w_kv = jnp.concatenate(
        [k_pad.reshape(A_KV_LORA, A_HEADS * HEAD_SLOT), w_uv.reshape(A_KV_LORA, A_HEADS * A_V)], axis=1).astype(BF16)

    eye = jnp.eye(A_HEADS, dtype=F32)
    uk_t = jnp.transpose(w_uk, (1, 2, 0))
    uk_t = jnp.concatenate([uk_t, jnp.zeros((A_HEADS, HEAD_SLOT - A_NOPE, A_KV_LORA), F32)], axis=1)
    absorb = (uk_t[:, :, None, :] * eye[:, None, :, None]).reshape(A_HEADS * HEAD_SLOT, A_HEADS * A_KV_LORA)
    sel = jnp.zeros((HEAD_SLOT, A_ROPE), F32).at[A_NOPE + jnp.arange(A_ROPE), jnp.arange(A_ROPE)].set(1.0)
    sel = (sel[None, :, None, :] * eye[:, None, :, None]).reshape(A_HEADS * HEAD_SLOT, A_HEADS * A_ROPE)
    w_dec_q = jnp.concatenate([absorb, sel], axis=1).astype(BF16)
    uv_t = jnp.transpose(w_uv, (1, 0, 2))
    w_dec_o = (uv_t[:, :, None, :] * eye[:, None, :, None]).reshape(A_HEADS * A_KV_LORA, A_HEADS * A_V).astype(BF16)

    n_a = A_HEADS * A_V
    return dict(w_in=w_in_p, g_q=g_q.reshape(1, -1), g_kv=g_kv.reshape(1, -1), w_q=w_q, w_kv=w_kv,
                w_dec_q=w_dec_q, w_dec_o=w_dec_o, w_out_a=w_out[:n_a].astype(BF16), w_out_b=w_out[n_a:].astype(BF16))


def _rope_tables(pos):
    inv = ROPE_BASE ** (-jnp.arange(0, A_ROPE, 2, dtype=F32) / A_ROPE)
    ang = pos.astype(F32)[:, None] * inv[None, :]
    cos, sin = jnp.cos(ang), jnp.sin(ang)
    n = pos.shape[0]
    zero = jnp.zeros_like(sin)
    head = jnp.zeros((n, A_NOPE), F32)
    tail = jnp.zeros((n, HEAD_SLOT - A_NOPE - A_ROPE), F32)
    cos_t = jnp.concatenate([jnp.ones((n, A_NOPE), F32), cos, cos, tail], axis=1)
    sin_lo = jnp.concatenate([head, -sin, zero, tail], axis=1)
    sin_hi = jnp.concatenate([head, zero, sin, tail], axis=1)
    c = SM_SCALE * LOG2_E
    return cos_t * c, sin_lo * c, sin_hi * c, cos_t, sin_lo, sin_hi


def kernel(x_prompt, x_sample, cache_ckv, cache_kpe, state_hgrn, state_conv, page_table, norm_g, final_norm_g,
           w_ffn_gate, w_ffn_up, w_ffn_down, w_in_ab, g_q_lora, g_kv_lora, w_uq, w_ukv, hgrn_lower_bound,
           g_hgrn_out, w_out_ab, w_in_c, w_conv_c, w_out_c):
    b_p, s_p, d = x_prompt.shape
    d_b, s_d, _ = x_sample.shape
    assert s_d == 1
    depth = norm_g.shape[0]
    past_len = page_table.shape[1] * cache_ckv.shape[2]
    n_p = b_p * s_p

    tab_p = _rope_tables(jnp.arange(s_p))
    tab_d = _rope_tables(jnp.full((d_b,), past_len))
    pool_kpe_t = jnp.swapaxes(cache_kpe, 2, 3)

    w_ffn = (w_ffn_gate.astype(BF16), w_ffn_up.astype(BF16), w_ffn_down.astype(BF16))

    xp = x_prompt.reshape(n_p, d)
    xd = x_sample.reshape(d_b, d)
    outs = {k: [] for k in ("ckv_p", "kpe_p", "ckv_d", "kpe_d", "hg_p", "hg_d", "cv_p", "cv_d")}

    for layer in range(depth):
        j = layer // 2
        last = layer == depth - 1
        xp = _ffn(xp, norm_g[layer, 0], w_ffn, (layer, 0))
        xd = _ffn(xd, norm_g[layer, 0], w_ffn, (layer, 0))
        ffn_b = (norm_g[layer, 2], w_ffn, (layer, 1))
        fin = final_norm_g if last else None
        if layer % 2 == 0:
            w = _prep_ab_weights(w_in_ab[j], g_q_lora[j], g_kv_lora[j], w_uq[j], w_ukv[j], w_out_ab[j])
            ckv, kpe_t, q, k, v, hq, hf, hi, hg = _ab_pre(xp, norm_g[layer, 1], w, tab_p, hgrn_lower_bound, layer)
            outs["ckv_p"].append(ckv.reshape(b_p, s_p, -1))
            outs["kpe_p"].append(jnp.swapaxes(kpe_t, 1, 2))
            o_a = _attn_prompt(q.reshape(b_p, s_p, -1), k.reshape(b_p, s_p, -1), v.reshape(b_p, s_p, -1))
            sh = (b_p, s_p, -1)
            hgrn_in = (hq.reshape(sh), hf.reshape(sh), hi.reshape(sh), hg.reshape(sh), g_hgrn_out[j])
            ckv, kpe_t, q, _, _, hq, hf, hi, hg = _ab_pre(xd, norm_g[layer, 1], w, tab_d, hgrn_lower_bound, layer)
            kpe = kpe_t[0].T
            qd = _small_matmul(q, w["w_dec_q"], F32)
            n_lat = A_HEADS * A_KV_LORA
            dec_in = (page_table, qd[:, :n_lat].reshape(d_b, A_HEADS, A_KV_LORA), qd[:, n_lat:].reshape(d_b, A_HEADS, A_ROPE),
                      ckv.reshape(d_b, 1, -1), kpe.reshape(d_b, 1, -1), cache_ckv, pool_kpe_t, j)
            ts = _row_tile(s_p, HGRN_ROWS_PER_STEP)
            n_chunk = _merged_chunks(b_p * (s_p // ts), d_b, page_table.shape[1], cache_ckv.shape[2])
            if n_chunk is None:
                o_b_p, st_p = _hgrn_prompt(*hgrn_in, rows_per_step=ts)
                o_lat = _attn_decode(*dec_in)
            else:
                o_b_p, st_p, o_lat = _hgrn_prompt_with_decode(*hgrn_in, *dec_in, n_chunk, ts)
            outs["hg_p"].append(st_p)
            pre = ((o_a.reshape(n_p, -1), w["w_out_a"]), (o_b_p.reshape(n_p, -1), w["w_out_b"]))
            xp = _ffn(xp, *ffn_b, pre=pre, final_g=fin)
            o_a = _small_matmul(o_lat.reshape(d_b, n_lat), w["w_dec_o"], BF16)
            o_b, st_d = _hgrn_step(hq, hf, hi, hg, g_hgrn_out[j], state_hgrn, j)
            xd = _ffn(xd, *ffn_b, pre=((o_a, w["w_out_a"]), (o_b, w["w_out_b"])), final_g=fin)
            outs["ckv_d"].append(ckv)
            outs["kpe_d"].append(kpe)
            outs["hg_d"].append(st_d)
        else:
            win = w_in_c[j].astype(BF16)
            wout = w_out_c[j].astype(BF16)
            xp3, buf_p = _cmix_prompt(xp.reshape(b_p, s_p, d), norm_g[layer, 1], win, w_conv_c[j], wout)
            xp = _ffn(xp3.reshape(n_p, d), *ffn_b, final_g=fin)
            xd, v_new = _cmix_step(xd, state_conv[j, :, 0], state_conv[j, :, 1], norm_g[layer, 1], win, w_conv_c[j], wout)
            xd = _ffn(xd, *ffn_b, final_g=fin)
            outs["cv_p"].append(buf_p)
            outs["cv_d"].append(jnp.stack([state_conv[j, :, 1], v_new], axis=1))

    return (xp.reshape(b_p, s_p, d), xd.reshape(d_b, s_d, d),
            jnp.stack(outs["ckv_p"]), jnp.stack(outs["kpe_p"]),
            jnp.stack(outs["ckv_d"]).reshape(-1, d_b, s_d, A_KV_LORA), jnp.stack(outs["kpe_d"]).reshape(-1, d_b, s_d, A_ROPE),
            jnp.stack(outs["hg_p"]), jnp.stack(outs["hg_d"]), jnp.stack(outs["cv_p"]), jnp.stack(outs["cv_d"]))
```

```python
import functools

import jax
import jax.numpy as jnp
from jax import lax
from jax.experimental import pallas as pl
from jax.experimental.pallas import tpu as pltpu

F32 = jnp.float32
BF16 = jnp.bfloat16

A_HEADS = 8
A_NOPE = 64
A_ROPE = 32
A_V = 64
A_Q_LORA = 384
A_KV_LORA = 256
ROPE_BASE = 10000.0
SM_SCALE = (A_NOPE + A_ROPE) ** -0.5
LOG2_E = 1.4426950408889634
B_HEADS = 4
B_DK = 128
B_DV = 128
B_CHUNK = 16
C_KERNEL = 3
RMS_EPS = 1e-6

LANES = 128
HEAD_SLOT = LANES
HGRN_BLOCK = 128
HGRN_ROWS_PER_STEP = 512
CMIX_SUB_ROWS = 256
ATTN_SCORES_AHEAD = 1
VMEM_LIMIT = 56 * 1024 * 1024
DECODE_BUFFER_BYTES = 40 * 1024 * 1024
DECODE_KEY_PARTS = 2


def _params(sem, vmem=VMEM_LIMIT):
    return pltpu.CompilerParams(dimension_semantics=sem, vmem_limit_bytes=vmem)


def _const_spec(shape):
    nd = len(shape)
    return pl.BlockSpec(shape, lambda *_: (0,) * nd, pipeline_mode=pl.Buffered(1))


def _rms(x, g):
    return x * lax.rsqrt(jnp.mean(x * x, axis=-1, keepdims=True) + RMS_EPS) * g


def _dot(a, b):
    return jnp.dot(a, b, preferred_element_type=F32)


def _dot_nt(a, b):
    return lax.dot_general(a, b, (((1,), (1,)), ((), ())), preferred_element_type=F32)


def _dot_tn(a, b):
    return lax.dot_general(a, b, (((0,), (0,)), ((), ())), preferred_element_type=F32)


def _silu(x):
    return x * jax.nn.sigmoid(x)


def _row_tile(n, want):
    t = min(n, want)
    assert n % t == 0, (n, t)
    return t


def _ffn_body(*refs, n_chunks, tf, n_pre, has_final):
    x_ref = refs[0]
    pre = refs[1:1 + 2 * n_pre]
    g_ref, wg_ref, wu_ref, wd_ref = refs[1 + 2 * n_pre:5 + 2 * n_pre]
    rest = refs[5 + 2 * n_pre:]
    gf_ref = rest[0] if has_final else None
    o_ref = rest[-1]

    x = x_ref[...]
    for p in range(n_pre):
        x = x + _dot(pre[2 * p][...], pre[2 * p + 1][...])
    h = _rms(x, g_ref[...]).astype(BF16)
    acc = None
    for c in range(n_chunks):
        cs = slice(c * tf, (c + 1) * tf)
        gate = _dot(h, wg_ref[:, cs])
        up = _dot(h, wu_ref[:, cs])
        a = (_silu(gate) * up).astype(BF16)
        part = _dot(a, wd_ref[cs, :])
        acc = part if acc is None else acc + part
    y = x + 0.5 * acc
    if has_final:
        y = _rms(y, gf_ref[...])
    o_ref[...] = y


def _ffn(x, g, weights, which, pre=(), final_g=None, tm=1024):
    n, d = x.shape
    f = weights[0].shape[-1]
    tf = 256 if f % 256 == 0 else LANES
    tm = _row_tile(n, tm)
    row = lambda i: (i, 0)
    in_specs = [pl.BlockSpec((tm, d), row)]
    args = [x]
    for act, w in pre:
        in_specs += [pl.BlockSpec((tm, act.shape[1]), row), _const_spec(w.shape)]
        args += [act, w]
    pick = lambda i: (*which, 0, 0)
    w_specs = [pl.BlockSpec((None, None) + w.shape[2:], pick, pipeline_mode=pl.Buffered(1)) for w in weights]
    in_specs += [_const_spec((1, d))] + w_specs
    args += [g.reshape(1, d), *weights]
    if final_g is not None:
        in_specs.append(_const_spec((1, d)))
        args.append(final_g.reshape(1, d))
    body = functools.partial(_ffn_body, n_chunks=f // tf, tf=tf, n_pre=len(pre), has_final=final_g is not None)
    return pl.pallas_call(
        body,
        grid=(n // tm,),
        in_specs=in_specs,
        out_specs=pl.BlockSpec((tm, d), row),
        out_shape=jax.ShapeDtypeStruct((n, d), F32),
        compiler_params=_params(("parallel",)),
        name="half_ffn",
    )(*args)


def _rope_slot(x, cos, sin_lo, sin_hi):
    half = A_ROPE // 2
    return x * cos + pltpu.roll(x, HEAD_SLOT - half, 1) * sin_lo + pltpu.roll(x, half, 1) * sin_hi


def _ab_pre_body(x_ref, g_ref, win_ref, gq_ref, gkv_ref, wq_ref, wkv_ref, cq_ref, sq_lo_ref, sq_hi_ref,
                 c_ref, s_lo_ref, s_hi_ref, lb_ref,
                 ckv_ref, kpet_ref, q_ref, k_ref, v_ref, hq_ref, hf_ref, hi_ref, hg_ref, *, layer):
    hb = B_HEADS * B_DK
    hw = A_HEADS * HEAD_SLOT
    o_kv = A_Q_LORA
    o_pe = o_kv + A_KV_LORA
    o_h = o_pe + HEAD_SLOT
    h = _rms(x_ref[...], g_ref[...]).astype(BF16)
    proj = _dot(h, win_ref[...])
    q_tabs = (cq_ref[...], sq_lo_ref[...], sq_hi_ref[...])

    cq = _rms(proj[:, :A_Q_LORA], gq_ref[...]).astype(BF16)
    qq = _dot(cq, wq_ref[...])
    for hd in range(A_HEADS):
        sl = slice(hd * HEAD_SLOT, (hd + 1) * HEAD_SLOT)
        q_ref[:, sl] = _rope_slot(qq[:, sl], *q_tabs).astype(BF16)

    ckv = _rms(proj[:, o_kv:o_pe], gkv_ref[...])
    ckv_ref[...] = ckv
    kpe = _rope_slot(proj[:, o_pe:o_h], c_ref[...], s_lo_ref[...], s_hi_ref[...])
    kpet_ref[0] = kpe.T[A_NOPE:A_NOPE + A_ROPE, :]
    kv = _dot(ckv.astype(BF16), wkv_ref[...])
    for hd in range(A_HEADS):
        sl = slice(hd * HEAD_SLOT, (hd + 1) * HEAD_SLOT)
        k_ref[:, sl] = (kv[:, sl] + kpe).astype(BF16)
    v_ref[...] = kv[:, hw:].astype(BF16)

    lbr = lb_ref[...]
    e = jnp.exp(lbr - jnp.max(lbr, axis=0, keepdims=True))
    lb = jnp.sum(e[:layer + 1], axis=0, keepdims=True) / jnp.sum(e, axis=0, keepdims=True)
    hq_ref[...] = _silu(proj[:, o_h:o_h + hb]) * (B_DK ** -0.5)
    hf_ref[...] = lb + (1.0 - lb) * jax.nn.sigmoid(proj[:, o_h + hb:o_h + 2 * hb])
    hi_ref[...] = proj[:, o_h + 2 * hb:o_h + 3 * hb].astype(BF16)
    hg_ref[...] = _silu(proj[:, o_h + 3 * hb:o_h + 4 * hb])


def _ab_pre(x, g, w, tables, lb_raw, layer, tm=1024):
    n, d = x.shape
    s_len = tables[0].shape[0]
    tm = _row_tile(s_len, tm)
    assert n % s_len == 0
    n_pos_tiles = s_len // tm
    row = lambda i: (i, 0)
    pos = lambda i: (i % n_pos_tiles, 0)
    hw = A_HEADS * HEAD_SLOT
    hb = B_HEADS * B_DK
    outs = [
        (A_KV_LORA, F32), None, (hw, BF16), (hw, BF16), (A_HEADS * A_V, BF16),
        (hb, F32), (hb, F32), (hb, BF16), (hb, F32),
    ]
    kpet_spec = pl.BlockSpec((1, A_ROPE, tm), lambda i: (i // n_pos_tiles, 0, i % n_pos_tiles))
    kpet_shape = jax.ShapeDtypeStruct((n // s_len, A_ROPE, s_len), F32)
    return pl.pallas_call(
        functools.partial(_ab_pre_body, layer=layer),
        grid=(n // tm,),
        in_specs=[
            pl.BlockSpec((tm, d), row), _const_spec((1, d)), _const_spec(w["w_in"].shape),
            _const_spec((1, A_Q_LORA)), _const_spec((1, A_KV_LORA)), _const_spec(w["w_q"].shape),
            _const_spec(w["w_kv"].shape)] + [pl.BlockSpec((tm, HEAD_SLOT), pos)] * len(tables) + [_const_spec(lb_raw.shape)],
        out_specs=[kpet_spec if o is None else pl.BlockSpec((tm, o[0]), row) for o in outs],
        out_shape=[kpet_shape if o is None else jax.ShapeDtypeStruct((n, o[0]), o[1]) for o in outs],
        compiler_params=_params(("parallel",)),
        name="ab_input_stage",
    )(x, g.reshape(1, d), w["w_in"], w["g_q"], w["g_kv"], w["w_q"], w["w_kv"], *tables, lb_raw)


def _attn_prompt_body(q_ref, k_ref, v_ref, o_ref, vaug, *, tq):
    s_len = q_ref.shape[1]
    nq = s_len // tq
    nv = 2 * A_V
    lane = lax.broadcasted_iota(jnp.int32, (tq, nv), 1)
    r_i = lax.broadcasted_iota(jnp.int32, (tq, tq), 0)
    c_i = lax.broadcasted_iota(jnp.int32, (tq, tq), 1)
    causal = c_i <= r_i
    vaug[:, :nv] = v_ref[0]
    vaug[:, nv:] = jnp.ones((s_len, nv), BF16)
    units = [(qi, hh) for qi in range(nq) for hh in range(2)]

    def scores(qi, hh):
        rows = slice(qi * tq, (qi + 1) * tq)
        hs = slice(hh * HEAD_SLOT, (hh + 1) * HEAD_SLOT)
        q = q_ref[0, rows, hs]
        s_d = jnp.where(causal, _dot_nt(q, k_ref[0, rows, hs]), -jnp.inf)
        s_p = _dot_nt(q, k_ref[0, :qi * tq, hs]) if qi else None
        return s_d, s_p

    def finish(qi, s_d, s_p):
        rows = slice(qi * tq, (qi + 1) * tq)
        m = jnp.max(s_d, axis=-1, keepdims=True)
        if s_p is not None:
            m = jnp.maximum(m, jnp.max(s_p, axis=-1, keepdims=True))
        o = _dot(jnp.exp2(s_d - m).astype(BF16), vaug[rows, :])
        if s_p is not None:
            o = o + _dot(jnp.exp2(s_p - m).astype(BF16), vaug[:qi * tq, :])
        return o[:, :nv] / o[:, nv:]

    pending, outs = {}, {}
    for idx in range(len(units) + ATTN_SCORES_AHEAD):
        if idx < len(units):
            pending[idx] = scores(*units[idx])
        j = idx - ATTN_SCORES_AHEAD
        if j >= 0:
            qi, hh = units[j]
            outs[hh] = finish(qi, *pending.pop(j))
            if hh == 1:
                o_ref[0, qi * tq:(qi + 1) * tq, :] = jnp.where(lane < A_V, outs[0], outs[1]).astype(BF16)


def _attn_prompt(q, k, v, tq=256):
    b, s, _ = q.shape
    tq = _row_tile(s, tq)
    blk2 = lambda i, j: (i, 0, j)
    return pl.pallas_call(
        functools.partial(_attn_prompt_body, tq=tq),
        grid=(b, A_HEADS // 2),
        in_specs=[
            pl.BlockSpec((1, s, 2 * HEAD_SLOT), blk2), pl.BlockSpec((1, s, 2 * HEAD_SLOT), blk2),
            pl.BlockSpec((1, s, 2 * A_V), blk2),
        ],
        out_specs=pl.BlockSpec((1, s, 2 * A_V), blk2),
        out_shape=jax.ShapeDtypeStruct((b, s, A_HEADS * A_V), BF16),
        scratch_shapes=[pltpu.VMEM((s, 4 * A_V), BF16)],
        compiler_params=_params(("parallel", "parallel")),
        name="mla_prompt_attention",
    )(q, k, v)


def _run_phases(phases):
    prologue, segments, epilogue = phases
    prologue()
    for seg in segments:
        seg()
    epilogue()


def _hgrn_prompt_body(*refs, n_sb):
    _run_phases(_hgrn_phases(*refs, n_sb=n_sb))


def _hgrn_phases(q_ref, f_ref, i_ref, gt_ref, go_ref, o_ref, st_ref, st_scr, *, n_sb):
    t = pl.program_id(1)

    def prologue():
        @pl.when(t == 0)
        def _():
            st_scr[...] = jnp.zeros_like(st_scr)

    def epilogue():
        @pl.when(t == pl.num_programs(1) - 1)
        def _():
            for hd in range(B_HEADS):
                st_ref[0, hd] = st_scr[hd].T

    prepared = {}
    preps = [functools.partial(_hgrn_prep, q_ref, f_ref, prepared, sb=sb) for sb in range(n_sb)]
    applies = [functools.partial(_hgrn_apply, i_ref, gt_ref, go_ref, o_ref, st_scr, prepared, sb=sb)
               for sb in range(n_sb)]
    return prologue, preps + applies, epilogue


def _hgrn_level_masks():
    R = HGRN_BLOCK
    r_i = lax.broadcasted_iota(jnp.int32, (R, R), 0)
    c_i = lax.broadcasted_iota(jnp.int32, (R, R), 1)
    levels = []
    m = B_CHUNK
    while 2 * m <= R:
        sh = (2 * m).bit_length() - 1
        levels.append((m, ((r_i >> sh) == (c_i >> sh)) & ((r_i & (2 * m - 1)) >= m) & ((c_i & (2 * m - 1)) < m)))
        m *= 2
    return r_i, c_i, levels


def _hgrn_prep(q_ref, f_ref, prepared, *, sb):
    R = HGRN_BLOCK
    hb = B_HEADS * B_DK
    r_i, c_i, levels = _hgrn_level_masks()
    tri = jnp.where(c_i <= r_i, 1.0, 0.0).astype(BF16)

    rows = slice(sb * R, (sb + 1) * R)
    q = q_ref[0, rows, :]
    f = f_ref[0, rows, :]
    k = 1.0 - f
    lf = jnp.log(f)
    hi = lf.astype(BF16)
    r1 = lf - hi.astype(F32)
    mid = r1.astype(BF16)
    lo = (r1 - mid.astype(F32)).astype(BF16)
    G = _dot(tri, hi) + _dot(tri, mid) + _dot(tri, lo)
    g_last = G[R - 1:R, :]

    n_ch = R // B_CHUNK
    end_rows = [G[B_CHUNK * (j + 1) - 1:B_CHUNK * (j + 1), :] for j in range(n_ch)]
    g_end = jnp.concatenate(end_rows, axis=0)
    g_start = jnp.concatenate([jnp.zeros((1, hb), F32)] + end_rows[:-1], axis=0)

    def expand(per_chunk):
        return jnp.concatenate([jnp.broadcast_to(per_chunk[j:j + 1, :], (B_CHUNK, hb)) for j in range(n_ch)], axis=0)

    g0 = expand(g_start)
    q0 = q * jnp.exp(G - g0)
    k_end = k * jnp.exp(expand(g_end) - G)
    q_lv = [q0.astype(BF16)]
    k_lv = [(k * jnp.exp(g0 - G)).astype(BF16)]
    q_int = (q0 * expand(jnp.exp(g_start))).astype(BF16)
    k_st = (k_end * expand(jnp.exp(g_last - g_end))).astype(BF16)
    for mm, _ in levels:
        cpb = 2 * mm // B_CHUNK
        g_mid = jnp.concatenate([end_rows[(j // cpb) * cpb + cpb // 2 - 1] for j in range(n_ch)], axis=0)
        q_lv.append((q0 * expand(jnp.exp(jnp.minimum(g_start - g_mid, 0.0)))).astype(BF16))
        k_lv.append((k_end * expand(jnp.exp(jnp.minimum(g_mid - g_end, 0.0)))).astype(BF16))
    prepared[sb] = (q_lv, k_lv, q_int, k_st, jnp.exp(g_last))


def _hgrn_apply(i_ref, gt_ref, go_ref, o_ref, st_scr, prepared, *, sb):
    R = HGRN_BLOCK
    r_i, c_i, levels = _hgrn_level_masks()
    base_mask = ((r_i >> 4) == (c_i >> 4)) & (c_i <= r_i)
    go = go_ref[...]
    rows = slice(sb * R, (sb + 1) * R)
    v = i_ref[0, rows, :]
    q_lv, k_lv, q_int, k_st, decay = prepared[sb]

    heads = [slice(hd * B_DK, (hd + 1) * B_DK) for hd in range(B_HEADS)]
    inter, state_in, atts = [], [], []
    for hd, hs in enumerate(heads):
        st = st_scr[hd]
        inter.append(_dot_nt(q_int[:, hs], st.astype(BF16)))
        state_in.append(st * decay[:, hs] + _dot_tn(v[:, hs], k_st[:, hs]))
    for hs in heads:
        att = jnp.where(base_mask, _dot_nt(q_lv[0][:, hs], k_lv[0][:, hs]), 0.0)
        for li, (_, msk) in enumerate(levels):
            att = jnp.where(msk, _dot_nt(q_lv[li + 1][:, hs], k_lv[li + 1][:, hs]), att)
        atts.append(att.astype(BF16))
    for hd, hs in enumerate(heads):
        st_scr[hd] = state_in[hd]
        o = _dot(atts[hd], v[:, hs]) + inter[hd]
        o = o * lax.rsqrt(jnp.mean(o * o, axis=-1, keepdims=True) + RMS_EPS) * go
        o_ref[0, rows, hs] = (o * gt_ref[0, rows, hs]).astype(BF16)


def _hgrn_prompt(hq, hf, hi, hg, g_out, rows_per_step):
    b, s, hb = hq.shape
    assert s % HGRN_BLOCK == 0
    ts = rows_per_step
    blk = lambda i, t: (i, t, 0)
    return pl.pallas_call(
        functools.partial(_hgrn_prompt_body, n_sb=ts // HGRN_BLOCK),
        grid=(b, s // ts),
        in_specs=[pl.BlockSpec((1, ts, hb), blk)] * 4 + [pl.BlockSpec((1, B_DV), lambda i, t: (0, 0))],
        out_specs=[
            pl.BlockSpec((1, ts, hb), blk),
            pl.BlockSpec((1, B_HEADS, B_DK, B_DV), lambda i, t: (i, 0, 0, 0)),
        ],
        out_shape=[
            jax.ShapeDtypeStruct((b, s, hb), BF16),
            jax.ShapeDtypeStruct((b, B_HEADS, B_DK, B_DV), F32),
        ],
        scratch_shapes=[pltpu.VMEM((B_HEADS, B_DV, B_DK), F32)],
        compiler_params=_params(("parallel", "arbitrary")),
        name="hgrn2_prompt",
    )(hq, hf, hi, hg, g_out.reshape(1, B_DV))


def _hgrn_step_body(q_ref, f_ref, i_ref, gt_ref, go_ref, s_ref, o_ref, so_ref, *, nb):
    go = go_ref[...]
    pad = jnp.zeros((B_DK - 3 * nb, B_DK), F32)
    for hd in range(B_HEADS):
        hs = slice(hd * B_DK, (hd + 1) * B_DK)
        f = f_ref[:, hs]
        cols = jnp.concatenate([q_ref[:, hs], f, 1.0 - f, pad], axis=0).T
        for j in range(nb):
            s_new = cols[:, nb + j:nb + j + 1] * s_ref[j, hd] + cols[:, 2 * nb + j:2 * nb + j + 1] * i_ref[j:j + 1, hs].astype(F32)
            so_ref[j, hd] = s_new
            o = jnp.sum(cols[:, j:j + 1] * s_new, axis=0, keepdims=True)
            o = o * lax.rsqrt(jnp.mean(o * o, axis=-1, keepdims=True) + RMS_EPS) * go
            o_ref[j:j + 1, hs] = (o * gt_ref[j:j + 1, hs]).astype(BF16)


def _hgrn_step(hq, hf, hi, hg, g_out, state, layer, nb=8):
    n, hb = hq.shape
    nb = _row_tile(n, nb)
    row = lambda i: (i, 0)
    st_in = lambda i: (layer, i, 0, 0, 0)
    st_out = lambda i: (i, 0, 0, 0)
    return pl.pallas_call(
        functools.partial(_hgrn_step_body, nb=nb),
        grid=(n // nb,),
        in_specs=[pl.BlockSpec((nb, hb), row)] * 4 + [_const_spec((1, B_DV)), pl.BlockSpec((None, nb, B_HEADS, B_DK, B_DV), st_in)],
        out_specs=[pl.BlockSpec((nb, hb), row), pl.BlockSpec((nb, B_HEADS, B_DK, B_DV), st_out)],
        out_shape=[jax.ShapeDtypeStruct((n, hb), BF16), jax.ShapeDtypeStruct(state.shape[1:], F32)],
        compiler_params=_params(("parallel",)),
        name="hgrn2_step",
    )(hq, hf, hi, hg, g_out.reshape(1, B_DV), state)


def _small_matmul_body(a_ref, w_ref, o_ref):
    o_ref[...] = _dot(a_ref[...].astype(BF16), w_ref[...]).astype(o_ref.dtype)


def _small_matmul(a, w, out_dtype):
    n = a.shape[0]
    out_spec = pl.BlockSpec((n, w.shape[1]), lambda i: (0, 0))
    return pl.pallas_call(
        _small_matmul_body,
        grid=(1,),
        in_specs=[_const_spec(a.shape), _const_spec(w.shape)],
        out_specs=out_spec,
        out_shape=jax.ShapeDtypeStruct((n, w.shape[1]), out_dtype),
        compiler_params=_params(("arbitrary",)),
        name="sample_matmul",
    )(a, w)


def _page_copies(pt_ref, ckv_hbm, kpe_hbm, ckv_buf, kpe_buf, sems, step, slot, *, n_chunk, pc, page):
    b = step // n_chunk
    c = step % n_chunk
    copies = []
    for p in range(pc):
        pg = pt_ref[b, c * pc + p]
        copies.append(pltpu.make_async_copy(ckv_hbm.at[pg], ckv_buf.at[slot, pl.ds(p * page, page), :], sems.at[slot, 0]))
        copies.append(pltpu.make_async_copy(kpe_hbm.at[pg], kpe_buf.at[slot, :, pl.ds(p * page, page)], sems.at[slot, 1]))
    return copies


def _attn_decode_body(pt_ref, *refs, **kw):
    _run_phases(_decode_phases(pl.program_id(0), pl.num_programs(0), pt_ref, *refs, **kw))


def _decode_phases(s, total, pt_ref, ql_ref, qp_ref, cn_ref, kn_ref, ckv_hbm, kpe_hbm, o_ref,
                   ckv_buf, kpe_buf, sems, m_scr, l_scr, acc_scr, *, layer, n_chunk, pc, page):
    slot = s % 2
    c = s % n_chunk
    copies = functools.partial(_page_copies, pt_ref, ckv_hbm.at[layer], kpe_hbm.at[layer], ckv_buf, kpe_buf, sems,
                               n_chunk=n_chunk, pc=pc, page=page)

    def prologue():
        @pl.when(s == 0)
        def _():
            for cp in copies(s, slot):
                cp.start()

        @pl.when(s + 1 < total)
        def _():
            for cp in copies(s + 1, 1 - slot):
                cp.start()

        for cp in copies(s, slot):
            cp.wait()

        @pl.when(c == 0)
        def _():
            cn = cn_ref[0]
            s_self = (jnp.sum(ql_ref[0] * cn, axis=-1, keepdims=True)
                      + jnp.sum(qp_ref[0] * kn_ref[0], axis=-1, keepdims=True))
            m_scr[...] = s_self
            l_scr[...] = jnp.ones_like(l_scr)
            acc_scr[...] = jnp.broadcast_to(cn, acc_scr.shape)

    n_keys = pc * page
    part = n_keys // DECODE_KEY_PARTS if n_keys % (DECODE_KEY_PARTS * LANES) == 0 else n_keys
    n_parts = n_keys // part
    scores, stats = {}, {}

    def part_scores(i):
        ks = slice(i * part, (i + 1) * part)
        scores[i] = _dot_nt(ql_ref[0], ckv_buf[slot, ks, :]) + _dot(qp_ref[0], kpe_buf[slot, :, ks])

    def part_pv(i):
        ks = slice(i * part, (i + 1) * part)
        m_i = jnp.max(scores[i], axis=-1, keepdims=True)
        p = jnp.exp2(scores[i] - m_i)
        stats[i] = (m_i, jnp.sum(p, axis=-1, keepdims=True), _dot(p, ckv_buf[slot, ks, :]))

    def merge():
        m_old = m_scr[...]
        m_new = m_old
        for i in range(n_parts):
            m_new = jnp.maximum(m_new, stats[i][0])
        alpha = jnp.exp2(m_old - m_new)
        l_new = alpha * l_scr[...]
        acc_new = alpha * acc_scr[...]
        for i in range(n_parts):
            m_i, l_i, acc_i = stats[i]
            w_i = jnp.exp2(m_i - m_new)
            l_new = l_new + w_i * l_i
            acc_new = acc_new + w_i * acc_i
        l_scr[...] = l_new
        acc_scr[...] = acc_new
        m_scr[...] = m_new

    def epilogue():
        @pl.when(c == n_chunk - 1)
        def _():
            o_ref[0] = acc_scr[...] / l_scr[...]

    segments = ([functools.partial(part_scores, i) for i in range(n_parts)]
                + [functools.partial(part_pv, i) for i in range(n_parts)] + [merge])
    return prologue, segments, epilogue


def _decode_pages_per_step(n_pages, page):
    budget = DECODE_BUFFER_BYTES // (2 * page * (A_KV_LORA + A_ROPE) * 4)
    return max(p for p in range(1, n_pages + 1) if n_pages % p == 0 and p <= max(budget, 1))


def _attn_decode(page_table, q_lat, q_pe, ckv_new, kpe_new, pool_ckv, pool_kpe_t, layer):
    db, n_pages = page_table.shape
    page = pool_ckv.shape[2]
    pc = _decode_pages_per_step(n_pages, page)
    n_chunk = n_pages // pc
    per_row = lambda s, pt: (s // n_chunk, 0, 0)
    grid_spec = pltpu.PrefetchScalarGridSpec(
        num_scalar_prefetch=1,
        grid=(db * n_chunk,),
        in_specs=[
            pl.BlockSpec((1, A_HEADS, A_KV_LORA), per_row), pl.BlockSpec((1, A_HEADS, A_ROPE), per_row),
            pl.BlockSpec((1, 1, A_KV_LORA), per_row), pl.BlockSpec((1, 1, A_ROPE), per_row),
            pl.BlockSpec(memory_space=pl.ANY), pl.BlockSpec(memory_space=pl.ANY),
        ],
        out_specs=pl.BlockSpec((1, A_HEADS, A_KV_LORA), per_row),
        scratch_shapes=_decode_scratch(pc, page),
    )
    return pl.pallas_call(
        functools.partial(_attn_decode_body, layer=layer, n_chunk=n_chunk, pc=pc, page=page),
        grid_spec=grid_spec,
        out_shape=jax.ShapeDtypeStruct((db, A_HEADS, A_KV_LORA), F32),
        compiler_params=_params(("arbitrary",)),
        name="mla_paged_decode",
    )(page_table, q_lat, q_pe, ckv_new, kpe_new, pool_ckv, pool_kpe_t)


def _decode_scratch(pc, page):
    return [
        pltpu.VMEM((2, pc * page, A_KV_LORA), F32), pltpu.VMEM((2, A_ROPE, pc * page), F32),
        pltpu.SemaphoreType.DMA((2, 2)),
        pltpu.VMEM((A_HEADS, 1), F32), pltpu.VMEM((A_HEADS, 1), F32), pltpu.VMEM((A_HEADS, A_KV_LORA), F32),
    ]


def _hgrn_decode_body(pt_ref, q_ref, f_ref, i_ref, gt_ref, go_ref, ql_ref, qp_ref, cn_ref, kn_ref, ckv_hbm, kpe_hbm,
                      o_ref, st_ref, ol_ref, st_scr, ckv_buf, kpe_buf, sems, m_scr, l_scr, acc_scr, *, n_sb, decode_kw):
    step = pl.program_id(0) * pl.num_programs(1) + pl.program_id(1)
    total = pl.num_programs(0) * pl.num_programs(1)
    dec_pro, dec_segs, dec_epi = _decode_phases(step, total, pt_ref, ql_ref, qp_ref, cn_ref, kn_ref, ckv_hbm, kpe_hbm,
                                                 ol_ref, ckv_buf, kpe_buf, sems, m_scr, l_scr, acc_scr, **decode_kw)
    hgr_pro, hgr_segs, hgr_epi = _hgrn_phases(q_ref, f_ref, i_ref, gt_ref, go_ref, o_ref, st_ref, st_scr, n_sb=n_sb)
    dec_pro()
    hgr_pro()
    n_prep = len(hgr_segs) // 2
    n_score = (len(dec_segs) - 1) // 2
    order = (dec_segs[:1] + hgr_segs[:n_prep] + dec_segs[1:n_score]
             + _interleave(dec_segs[n_score:], hgr_segs[n_prep:]))
    for seg in order:
        seg()
    dec_epi()
    hgr_epi()


def _interleave(a, b):
    out, j = [], 0
    for i, seg in enumerate(a):
        out.append(seg)
        upto = (i + 1) * len(b) // len(a)
        out.extend(b[j:upto])
        j = upto
    return out


def _merged_chunks(n_steps, db, n_pages, page):
    if n_steps % db:
        return None
    n_chunk = n_steps // db
    if n_pages % n_chunk or n_pages // n_chunk > _decode_pages_per_step(n_pages, page):
        return None
    return n_chunk


def _hgrn_prompt_with_decode(hq, hf, hi, hg, g_out, page_table, q_lat, q_pe, ckv_new, kpe_new, pool_ckv, pool_kpe_t,
                             layer, n_chunk, rows_per_step):
    b, s, hb = hq.shape
    ts = rows_per_step
    nt = s // ts
    db, n_pages = page_table.shape
    page = pool_ckv.shape[2]
    pc = n_pages // n_chunk
    blk = lambda i, t, pt: (i, t, 0)
    per_row = lambda i, t, pt: ((i * nt + t) // n_chunk, 0, 0)
    grid_spec = pltpu.PrefetchScalarGridSpec(
        num_scalar_prefetch=1,
        grid=(b, nt),
        in_specs=[pl.BlockSpec((1, ts, hb), blk)] * 4 + [
            pl.BlockSpec((1, B_DV), lambda i, t, pt: (0, 0)),
            pl.BlockSpec((1, A_HEADS, A_KV_LORA), per_row), pl.BlockSpec((1, A_HEADS, A_ROPE), per_row),
            pl.BlockSpec((1, 1, A_KV_LORA), per_row), pl.BlockSpec((1, 1, A_ROPE), per_row),
            pl.BlockSpec(memory_space=pl.ANY), pl.BlockSpec(memory_space=pl.ANY),
        ],
        out_specs=[
            pl.BlockSpec((1, ts, hb), blk),
            pl.BlockSpec((1, B_HEADS, B_DK, B_DV), lambda i, t, pt: (i, 0, 0, 0)),
            pl.BlockSpec((1, A_HEADS, A_KV_LORA), per_row),
        ],
        scratch_shapes=[pltpu.VMEM((B_HEADS, B_DV, B_DK), F32)] + _decode_scratch(pc, page),
    )
    decode_kw = dict(layer=layer, n_chunk=n_chunk, pc=pc, page=page)
    return pl.pallas_call(
        functools.partial(_hgrn_decode_body, n_sb=ts // HGRN_BLOCK, decode_kw=decode_kw),
        grid_spec=grid_spec,
        out_shape=[
            jax.ShapeDtypeStruct((b, s, hb), BF16),
            jax.ShapeDtypeStruct((b, B_HEADS, B_DK, B_DV), F32),
            jax.ShapeDtypeStruct((db, A_HEADS, A_KV_LORA), F32),
        ],
        compiler_params=_params(("arbitrary", "arbitrary")),
        name="hgrn2_prompt_with_decode",
    )(page_table, hq, hf, hi, hg, g_out.reshape(1, B_DV), q_lat, q_pe, ckv_new, kpe_new, pool_ckv, pool_kpe_t)


def _cmix_prompt_body(x_ref, g_ref, win_ref, wc_ref, wout_ref, o_ref, buf_ref, carry, *, n_tiles):
    t = pl.program_id(1)
    tm = x_ref.shape[1]
    cw = wout_ref.shape[0]

    @pl.when(t == 0)
    def _():
        carry[...] = jnp.zeros_like(carry)

    ts = min(tm, CMIX_SUB_ROWS)
    n_sub = tm // ts

    def project(i):
        x = x_ref[0, i * ts:(i + 1) * ts, :]
        return x, _dot(_rms(x, g_ref[...]).astype(BF16), win_ref[...])

    row = lax.broadcasted_iota(jnp.int32, (ts, cw), 0)
    prev2, prev1 = carry[0:1, :], carry[1:2, :]
    nxt = project(0)
    for i in range(n_sub):
        x, p = nxt
        if i + 1 < n_sub:
            nxt = project(i + 1)
        v = p[:, cw:2 * cw] * p[:, 2 * cw:]
        b1 = jnp.broadcast_to(prev1, v.shape)
        v1 = jnp.where(row == 0, b1, pltpu.roll(v, 1, 0))
        v2 = jnp.where(row == 0, jnp.broadcast_to(prev2, v.shape), jnp.where(row == 1, b1, pltpu.roll(v, 2, 0)))
        y = wc_ref[0:1, :] * v2 + wc_ref[1:2, :] * v1 + wc_ref[2:3, :] * v
        o_ref[0, i * ts:(i + 1) * ts, :] = x + _dot((p[:, :cw] * y).astype(BF16), wout_ref[...])
        prev2, prev1 = v[ts - 2:ts - 1, :], v[ts - 1:ts, :]
    last2 = jnp.concatenate([prev2, prev1], axis=0)
    carry[0:2, :] = last2

    @pl.when(t == n_tiles - 1)
    def _():
        buf_ref[0] = last2


def _cmix_prompt(x, g, w_in, w_conv, w_out, tm=1024):
    b, s, d = x.shape
    tm = _row_tile(s, tm)
    n_tiles = s // tm
    cw = w_out.shape[0]
    blk = lambda i, t: (i, t, 0)
    cst = lambda shape: pl.BlockSpec(shape, lambda i, t: (0,) * len(shape), pipeline_mode=pl.Buffered(1))
    return pl.pallas_call(
        functools.partial(_cmix_prompt_body, n_tiles=n_tiles),
        grid=(b, n_tiles),
        in_specs=[pl.BlockSpec((1, tm, d), blk), cst((1, d)), cst(w_in.shape), cst(w_conv.shape), cst(w_out.shape)],
        out_specs=[pl.BlockSpec((1, tm, d), blk), pl.BlockSpec((1, C_KERNEL - 1, cw), lambda i, t: (i, 0, 0))],
        out_shape=[jax.ShapeDtypeStruct((b, s, d), F32), jax.ShapeDtypeStruct((b, C_KERNEL - 1, cw), F32)],
        scratch_shapes=[pltpu.VMEM((8, cw), F32)],
        compiler_params=_params(("parallel", "arbitrary")),
        name="conv_mixer_prompt",
    )(x, g.reshape(1, d), w_in, w_conv, w_out)


def _cmix_step_body(x_ref, b0_ref, b1_ref, g_ref, win_ref, wc_ref, wout_ref, o_ref, v_ref):
    cw = wout_ref.shape[0]
    x = x_ref[...]
    h = _rms(x, g_ref[...]).astype(BF16)
    p = _dot(h, win_ref[...])
    v = p[:, cw:2 * cw] * p[:, 2 * cw:]
    y = wc_ref[0:1, :] * b0_ref[...] + wc_ref[1:2, :] * b1_ref[...] + wc_ref[2:3, :] * v
    o_ref[...] = x + _dot((p[:, :cw] * y).astype(BF16), wout_ref[...])
    v_ref[...] = v


def _cmix_step(x, buf0, buf1, g, w_in, w_conv, w_out):
    n, d = x.shape
    cw = w_out.shape[0]
    args = (x, buf0, buf1, g.reshape(1, d), w_in, w_conv, w_out)
    return pl.pallas_call(
        _cmix_step_body,
        grid=(1,),
        in_specs=[_const_spec(a.shape) for a in args],
        out_specs=[pl.BlockSpec((n, d), lambda i: (0, 0)), pl.BlockSpec((n, cw), lambda i: (0, 0))],
        out_shape=[jax.ShapeDtypeStruct((n, d), F32), jax.ShapeDtypeStruct((n, cw), F32)],
        compiler_params=_params(("arbitrary",)),
        name="conv_mixer_step",
    )(*args)


def _prep_ab_weights(w_in, g_q, g_kv, w_uq, w_ukv, w_out):
    d = w_in.shape[0]
    o_kv = A_Q_LORA
    o_pe = o_kv + A_KV_LORA
    o_h = o_pe + A_ROPE
    zn = jnp.zeros((d, A_NOPE), F32)
    zt = jnp.zeros((d, HEAD_SLOT - A_NOPE - A_ROPE), F32)
    w_in_p = jnp.concatenate([w_in[:, :o_pe], zn, w_in[:, o_pe:o_h], zt, w_in[:, o_h:]], axis=1).astype(BF16)

    wq = w_uq.reshape(A_Q_LORA, A_HEADS, A_NOPE + A_ROPE)
    zq = jnp.zeros((A_Q_LORA, A_HEADS, HEAD_SLOT - A_NOPE - A_ROPE), F32)
    w_q = jnp.concatenate([wq, zq], axis=-1).reshape(A_Q_LORA, A_HEADS * HEAD_SLOT).astype(BF16)

    wkv = w_ukv.reshape(A_KV_LORA, A_HEADS, A_NOPE + A_V)
    w_uk, w_uv = wkv[..., :A_NOPE], wkv[..., A_NOPE:]
    k_pad = jnp.concatenate([w_uk, jnp.zeros((A_KV_LORA, A_HEADS, HEAD_SLOT - A_NOPE), F32)], axis=-1)
    w_kv = jnp.concatenate(
        [k_pad.reshape(A_KV_LORA, A_HEADS * HEAD_SLOT), w_uv.reshape(A_KV_LORA, A_HEADS * A_V)], axis=1).astype(BF16)

    eye = jnp.eye(A_HEADS, dtype=F32)
    uk_t = jnp.transpose(w_uk, (1, 2, 0))
    uk_t = jnp.concatenate([uk_t, jnp.zeros((A_HEADS, HEAD_SLOT - A_NOPE, A_KV_LORA), F32)], axis=1)
    absorb = (uk_t[:, :, None, :] * eye[:, None, :, None]).reshape(A_HEADS * HEAD_SLOT, A_HEADS * A_KV_LORA)
    sel = jnp.zeros((HEAD_SLOT, A_ROPE), F32).at[A_NOPE + jnp.arange(A_ROPE), jnp.arange(A_ROPE)].set(1.0)
    sel = (sel[None, :, None, :] * eye[:, None, :, None]).reshape(A_HEADS * HEAD_SLOT, A_HEADS * A_ROPE)
    w_dec_q = jnp.concatenate([absorb, sel], axis=1).astype(BF16)
    uv_t = jnp.transpose(w_uv, (1, 0, 2))
    w_dec_o = (uv_t[:, :, None, :] * eye[:, None, :, None]).reshape(A_HEADS * A_KV_LORA, A_HEADS * A_V).astype(BF16)

    n_a = A_HEADS * A_V
    return dict(w_in=w_in_p, g_q=g_q.reshape(1, -1), g_kv=g_kv.reshape(1, -1), w_q=w_q, w_kv=w_kv,
                w_dec_q=w_dec_q, w_dec_o=w_dec_o, w_out_a=w_out[:n_a].astype(BF16), w_out_b=w_out[n_a:].astype(BF16))


def _rope_tables(pos):
    inv = ROPE_BASE ** (-jnp.arange(0, A_ROPE, 2, dtype=F32) / A_ROPE)
    ang = pos.astype(F32)[:, None] * inv[None, :]
    cos, sin = jnp.cos(ang), jnp.sin(ang)
    n = pos.shape[0]
    zero = jnp.zeros_like(sin)
    head = jnp.zeros((n, A_NOPE), F32)
    tail = jnp.zeros((n, HEAD_SLOT - A_NOPE - A_ROPE), F32)
    cos_t = jnp.concatenate([jnp.ones((n, A_NOPE), F32), cos, cos, tail], axis=1)
    sin_lo = jnp.concatenate([head, -sin, zero, tail], axis=1)
    sin_hi = jnp.concatenate([head, zero, sin, tail], axis=1)
    c = SM_SCALE * LOG2_E
    return cos_t * c, sin_lo * c, sin_hi * c, cos_t, sin_lo, sin_hi


def kernel(x_prompt, x_sample, cache_ckv, cache_kpe, state_hgrn, state_conv, page_table, norm_g, final_norm_g,
           w_ffn_gate, w_ffn_up, w_ffn_down, w_in_ab, g_q_lora, g_kv_lora, w_uq, w_ukv, hgrn_lower_bound,
           g_hgrn_out, w_out_ab, w_in_c, w_conv_c, w_out_c):
    b_p, s_p, d = x_prompt.shape
    d_b, s_d, _ = x_sample.shape
    assert s_d == 1
    depth = norm_g.shape[0]
    past_len = page_table.shape[1] * cache_ckv.shape[2]
    n_p = b_p * s_p

    tab_p = _rope_tables(jnp.arange(s_p))
    tab_d = _rope_tables(jnp.full((d_b,), past_len))
    pool_kpe_t = jnp.swapaxes(cache_kpe, 2, 3)

    w_ffn = (w_ffn_gate.astype(BF16), w_ffn_up.astype(BF16), w_ffn_down.astype(BF16))

    xp = x_prompt.reshape(n_p, d)
    xd = x_sample.reshape(d_b, d)
    outs = {k: [] for k in ("ckv_p", "kpe_p", "ckv_d", "kpe_d", "hg_p", "hg_d", "cv_p", "cv_d")}

    for layer in range(depth):
        j = layer // 2
        last = layer == depth - 1
        xp = _ffn(xp, norm_g[layer, 0], w_ffn, (layer, 0))
        xd = _ffn(xd, norm_g[layer, 0], w_ffn, (layer, 0))
        ffn_b = (norm_g[layer, 2], w_ffn, (layer, 1))
        fin = final_norm_g if last else None
        if layer % 2 == 0:
            w = _prep_ab_weights(w_in_ab[j], g_q_lora[j], g_kv_lora[j], w_uq[j], w_ukv[j], w_out_ab[j])
            ckv, kpe_t, q, k, v, hq, hf, hi, hg = _ab_pre(xp, norm_g[layer, 1], w, tab_p, hgrn_lower_bound, layer)
            outs["ckv_p"].append(ckv.reshape(b_p, s_p, -1))
            outs["kpe_p"].append(jnp.swapaxes(kpe_t, 1, 2))
            o_a = _attn_prompt(q.reshape(b_p, s_p, -1), k.reshape(b_p, s_p, -1), v.reshape(b_p, s_p, -1))
            sh = (b_p, s_p, -1)
            hgrn_in = (hq.reshape(sh), hf.reshape(sh), hi.reshape(sh), hg.reshape(sh), g_hgrn_out[j])
            ckv, kpe_t, q, _, _, hq, hf, hi, hg = _ab_pre(xd, norm_g[layer, 1], w, tab_d, hgrn_lower_bound, layer)
            kpe = kpe_t[0].T
            qd = _small_matmul(q, w["w_dec_q"], F32)
            n_lat = A_HEADS * A_KV_LORA
            dec_in = (page_table, qd[:, :n_lat].reshape(d_b, A_HEADS, A_KV_LORA), qd[:, n_lat:].reshape(d_b, A_HEADS, A_ROPE),
                      ckv.reshape(d_b, 1, -1), kpe.reshape(d_b, 1, -1), cache_ckv, pool_kpe_t, j)
            ts = _row_tile(s_p, HGRN_ROWS_PER_STEP)
            n_chunk = _merged_chunks(b_p * (s_p // ts), d_b, page_table.shape[1], cache_ckv.shape[2])
            if n_chunk is None:
                o_b_p, st_p = _hgrn_prompt(*hgrn_in, rows_per_step=ts)
                o_lat = _attn_decode(*dec_in)
            else:
                o_b_p, st_p, o_lat = _hgrn_prompt_with_decode(*hgrn_in, *dec_in, n_chunk, ts)
            outs["hg_p"].append(st_p)
            pre = ((o_a.reshape(n_p, -1), w["w_out_a"]), (o_b_p.reshape(n_p, -1), w["w_out_b"]))
            xp = _ffn(xp, *ffn_b, pre=pre, final_g=fin)
            o_a = _small_matmul(o_lat.reshape(d_b, n_lat), w["w_dec_o"], BF16)
            o_b, st_d = _hgrn_step(hq, hf, hi, hg, g_hgrn_out[j], state_hgrn, j)
            xd = _ffn(xd, *ffn_b, pre=((o_a, w["w_out_a"]), (o_b, w["w_out_b"])), final_g=fin)
            outs["ckv_d"].append(ckv)
            outs["kpe_d"].append(kpe)
            outs["hg_d"].append(st_d)
        else:
            win = w_in_c[j].astype(BF16)
            wout = w_out_c[j].astype(BF16)
            xp3, buf_p = _cmix_prompt(xp.reshape(b_p, s_p, d), norm_g[layer, 1], win, w_conv_c[j], wout)
            xp = _ffn(xp3.reshape(n_p, d), *ffn_b, final_g=fin)
            xd, v_new = _cmix_step(xd, state_conv[j, :, 0], state_conv[j, :, 1], norm_g[layer, 1], win, w_conv_c[j], wout)
            xd = _ffn(xd, *ffn_b, final_g=fin)
            outs["cv_p"].append(buf_p)
            outs["cv_d"].append(jnp.stack([state_conv[j, :, 1], v_new], axis=1))

    return (xp.reshape(b_p, s_p, d), xd.reshape(d_b, s_d, d),
            jnp.stack(outs["ckv_p"]), jnp.stack(outs["kpe_p"]),
            jnp.stack(outs["ckv_d"]).reshape(-1, d_b, s_d, A_KV_LORA), jnp.stack(outs["kpe_d"]).reshape(-1, d_b, s_d, A_ROPE),
            jnp.stack(outs["hg_p"]), jnp.stack(outs["hg_d"]), jnp.stack(outs["cv_p"]), jnp.stack(outs["cv_d"]))
```

```python
import functools

import jax
import jax.numpy as jnp
from jax import lax
from jax.experimental import pallas as pl
from jax.experimental.pallas import tpu as pltpu

F32 = jnp.float32
BF16 = jnp.bfloat16

A_HEADS = 8
A_NOPE = 64
A_ROPE = 32
A_V = 64
A_Q_LORA = 384
A_KV_LORA = 256
ROPE_BASE = 10000.0
SM_SCALE = (A_NOPE + A_ROPE) ** -0.5
LOG2_E = 1.4426950408889634
B_HEADS = 4
B_DK = 128
B_DV = 128
B_CHUNK = 16
C_KERNEL = 3
RMS_EPS = 1e-6

LANES = 128
HEAD_SLOT = LANES
HGRN_BLOCK = 128
HGRN_ROWS_PER_STEP = 512
ATTN_HEAD_PAIRS = 4
ATTN_SCORES_AHEAD = 1
VMEM_LIMIT = 56 * 1024 * 1024
DECODE_BUFFER_BYTES = 40 * 1024 * 1024
DECODE_KEY_PARTS = 2


def _params(sem, vmem=VMEM_LIMIT):
    return pltpu.CompilerParams(dimension_semantics=sem, vmem_limit_bytes=vmem)


def _const_spec(shape):
    nd = len(shape)
    return pl.BlockSpec(shape, lambda *_: (0,) * nd, pipeline_mode=pl.Buffered(1))


def _rms(x, g):
    return x * lax.rsqrt(jnp.mean(x * x, axis=-1, keepdims=True) + RMS_EPS) * g


def _dot(a, b):
    return jnp.dot(a, b, preferred_element_type=F32)


def _dot_nt(a, b):
    return lax.dot_general(a, b, (((1,), (1,)), ((), ())), preferred_element_type=F32)


def _dot_tn(a, b):
    return lax.dot_general(a, b, (((0,), (0,)), ((), ())), preferred_element_type=F32)


def _silu(x):
    return x * jax.nn.sigmoid(x)


def _row_tile(n, want):
    t = min(n, want)
    assert n % t == 0, (n, t)
    return t


def _ffn_body(*refs, n_chunks, tf, n_pre, has_final):
    x_ref = refs[0]
    pre = refs[1:1 + 2 * n_pre]
    g_ref, wg_ref, wu_ref, wd_ref = refs[1 + 2 * n_pre:5 + 2 * n_pre]
    rest = refs[5 + 2 * n_pre:]
    gf_ref = rest[0] if has_final else None
    o_ref = rest[-1]

    x = x_ref[...]
    for p in range(n_pre):
        x = x + _dot(pre[2 * p][...], pre[2 * p + 1][...])
    h = _rms(x, g_ref[...]).astype(BF16)
    acc = None
    for c in range(n_chunks):
        cs = slice(c * tf, (c + 1) * tf)
        gate = _dot(h, wg_ref[:, cs])
        up = _dot(h, wu_ref[:, cs])
        a = (_silu(gate) * up).astype(BF16)
        part = _dot(a, wd_ref[cs, :])
        acc = part if acc is None else acc + part
    y = x + 0.5 * acc
    if has_final:
        y = _rms(y, gf_ref[...])
    o_ref[...] = y


def _ffn(x, g, weights, which, pre=(), final_g=None, tm=1024):
    n, d = x.shape
    f = weights[0].shape[-1]
    tf = 256 if f % 256 == 0 else LANES
    tm = _row_tile(n, tm)
    row = lambda i: (i, 0)
    in_specs = [pl.BlockSpec((tm, d), row)]
    args = [x]
    for act, w in pre:
        in_specs += [pl.BlockSpec((tm, act.shape[1]), row), _const_spec(w.shape)]
        args += [act, w]
    pick = lambda i: (*which, 0, 0)
    w_specs = [pl.BlockSpec((None, None) + w.shape[2:], pick, pipeline_mode=pl.Buffered(1)) for w in weights]
    in_specs += [_const_spec((1, d))] + w_specs
    args += [g.reshape(1, d), *weights]
    if final_g is not None:
        in_specs.append(_const_spec((1, d)))
        args.append(final_g.reshape(1, d))
    body = functools.partial(_ffn_body, n_chunks=f // tf, tf=tf, n_pre=len(pre), has_final=final_g is not None)
    return pl.pallas_call(
        body,
        grid=(n // tm,),
        in_specs=in_specs,
        out_specs=pl.BlockSpec((tm, d), row),
        out_shape=jax.ShapeDtypeStruct((n, d), F32),
        compiler_params=_params(("parallel",)),
        name="half_ffn",
    )(*args)


def _rope_slot(x, cos, sin_lo, sin_hi):
    half = A_ROPE // 2
    return x * cos + pltpu.roll(x, HEAD_SLOT - half, 1) * sin_lo + pltpu.roll(x, half, 1) * sin_hi


def _ab_pre_body(x_ref, g_ref, win_ref, gq_ref, gkv_ref, wq_ref, wkv_ref, cq_ref, sq_lo_ref, sq_hi_ref,
                 c_ref, s_lo_ref, s_hi_ref, lb_ref,
                 ckv_ref, kpet_ref, q_ref, k_ref, v_ref, hq_ref, hf_ref, hi_ref, hg_ref, *, layer):
    hb = B_HEADS * B_DK
    hw = A_HEADS * HEAD_SLOT
    o_kv = A_Q_LORA
    o_pe = o_kv + A_KV_LORA
    o_h = o_pe + HEAD_SLOT
    h = _rms(x_ref[...], g_ref[...]).astype(BF16)
    proj = _dot(h, win_ref[...])
    q_tabs = (cq_ref[...], sq_lo_ref[...], sq_hi_ref[...])

    cq = _rms(proj[:, :A_Q_LORA], gq_ref[...]).astype(BF16)
    qq = _dot(cq, wq_ref[...])
    for hd in range(A_HEADS):
        sl = slice(hd * HEAD_SLOT, (hd + 1) * HEAD_SLOT)
        q_ref[:, sl] = _rope_slot(qq[:, sl], *q_tabs).astype(BF16)

    ckv = _rms(proj[:, o_kv:o_pe], gkv_ref[...])
    ckv_ref[...] = ckv
    kpe = _rope_slot(proj[:, o_pe:o_h], c_ref[...], s_lo_ref[...], s_hi_ref[...])
    kpet_ref[0] = kpe.T[A_NOPE:A_NOPE + A_ROPE, :]
    kv = _dot(ckv.astype(BF16), wkv_ref[...])
    for hd in range(A_HEADS):
        sl = slice(hd * HEAD_SLOT, (hd + 1) * HEAD_SLOT)
        k_ref[:, sl] = (kv[:, sl] + kpe).astype(BF16)
    v_ref[...] = kv[:, hw:].astype(BF16)

    lbr = lb_ref[...]
    e = jnp.exp(lbr - jnp.max(lbr, axis=0, keepdims=True))
    lb = jnp.sum(e[:layer + 1], axis=0, keepdims=True) / jnp.sum(e, axis=0, keepdims=True)
    hq_ref[...] = _silu(proj[:, o_h:o_h + hb]) * (B_DK ** -0.5)
    hf_ref[...] = lb + (1.0 - lb) * jax.nn.sigmoid(proj[:, o_h + hb:o_h + 2 * hb])
    hi_ref[...] = proj[:, o_h + 2 * hb:o_h + 3 * hb].astype(BF16)
    hg_ref[...] = _silu(proj[:, o_h + 3 * hb:o_h + 4 * hb])


def _ab_pre(x, g, w, tables, lb_raw, layer, tm=1024):
    n, d = x.shape
    s_len = tables[0].shape[0]
    tm = _row_tile(s_len, tm)
    assert n % s_len == 0
    n_pos_tiles = s_len // tm
    row = lambda i: (i, 0)
    pos = lambda i: (i % n_pos_tiles, 0)
    hw = A_HEADS * HEAD_SLOT
    hb = B_HEADS * B_DK
    outs = [
        (A_KV_LORA, F32), None, (hw, BF16), (hw, BF16), (A_HEADS * A_V, BF16),
        (hb, F32), (hb, F32), (hb, BF16), (hb, F32),
    ]
    kpet_spec = pl.BlockSpec((1, A_ROPE, tm), lambda i: (i // n_pos_tiles, 0, i % n_pos_tiles))
    kpet_shape = jax.ShapeDtypeStruct((n // s_len, A_ROPE, s_len), F32)
    return pl.pallas_call(
        functools.partial(_ab_pre_body, layer=layer),
        grid=(n // tm,),
        in_specs=[
            pl.BlockSpec((tm, d), row), _const_spec((1, d)), _const_spec(w["w_in"].shape),
            _const_spec((1, A_Q_LORA)), _const_spec((1, A_KV_LORA)), _const_spec(w["w_q"].shape),
            _const_spec(w["w_kv"].shape)] + [pl.BlockSpec((tm, HEAD_SLOT), pos)] * len(tables) + [_const_spec(lb_raw.shape)],
        out_specs=[kpet_spec if o is None else pl.BlockSpec((tm, o[0]), row) for o in outs],
        out_shape=[kpet_shape if o is None else jax.ShapeDtypeStruct((n, o[0]), o[1]) for o in outs],
        compiler_params=_params(("parallel",)),
        name="ab_input_stage",
    )(x, g.reshape(1, d), w["w_in"], w["g_q"], w["g_kv"], w["w_q"], w["w_kv"], *tables, lb_raw)


def _attn_prompt_body(q_ref, k_ref, v_ref, o_ref, vaug, *, tq, n_pairs):
    s_len = q_ref.shape[1]
    nq = s_len // tq
    nv = 2 * A_V
    lane = lax.broadcasted_iota(jnp.int32, (tq, nv), 1)
    r_i = lax.broadcasted_iota(jnp.int32, (tq, tq), 0)
    c_i = lax.broadcasted_iota(jnp.int32, (tq, tq), 1)
    causal = c_i <= r_i
    for pr in range(n_pairs):
        vaug[pr, :, :nv] = v_ref[0, :, pr * nv:(pr + 1) * nv]
        vaug[pr, :, nv:] = jnp.ones((s_len, nv), BF16)
    units = [(pr, qi, hh) for pr in range(n_pairs) for qi in range(nq) for hh in range(2)]

    def scores(pr, qi, hh):
        rows = slice(qi * tq, (qi + 1) * tq)
        hs = slice((2 * pr + hh) * HEAD_SLOT, (2 * pr + hh + 1) * HEAD_SLOT)
        q = q_ref[0, rows, hs]
        s_d = jnp.where(causal, _dot_nt(q, k_ref[0, rows, hs]), -jnp.inf)
        s_p = _dot_nt(q, k_ref[0, :qi * tq, hs]) if qi else None
        return s_d, s_p

    def finish(pr, qi, s_d, s_p):
        rows = slice(qi * tq, (qi + 1) * tq)
        m = jnp.max(s_d, axis=-1, keepdims=True)
        if s_p is not None:
            m = jnp.maximum(m, jnp.max(s_p, axis=-1, keepdims=True))
        o = _dot(jnp.exp2(s_d - m).astype(BF16), vaug[pr, rows, :])
        if s_p is not None:
            o = o + _dot(jnp.exp2(s_p - m).astype(BF16), vaug[pr, :qi * tq, :])
        return o[:, :nv] / o[:, nv:]

    pending, outs = {}, {}
    for idx in range(len(units) + ATTN_SCORES_AHEAD):
        if idx < len(units):
            pending[idx] = scores(*units[idx])
        j = idx - ATTN_SCORES_AHEAD
        if j >= 0:
            pr, qi, hh = units[j]
            outs[hh] = finish(pr, qi, *pending.pop(j))
            if hh == 1:
                o_ref[0, qi * tq:(qi + 1) * tq, pr * nv:(pr + 1) * nv] = (
                    jnp.where(lane < A_V, outs[0], outs[1]).astype(BF16))


def _attn_prompt(q, k, v, tq=256, n_pairs=ATTN_HEAD_PAIRS):
    b, s, _ = q.shape
    tq = _row_tile(s, tq)
    hps = 2 * n_pairs
    blk = lambda i, j: (i, 0, j)
    return pl.pallas_call(
        functools.partial(_attn_prompt_body, tq=tq, n_pairs=n_pairs),
        grid=(b, A_HEADS // hps),
        in_specs=[
            pl.BlockSpec((1, s, hps * HEAD_SLOT), blk), pl.BlockSpec((1, s, hps * HEAD_SLOT), blk),
            pl.BlockSpec((1, s, hps * A_V), blk),
        ],
        out_specs=pl.BlockSpec((1, s, hps * A_V), blk),
        out_shape=jax.ShapeDtypeStruct((b, s, A_HEADS * A_V), BF16),
        scratch_shapes=[pltpu.VMEM((n_pairs, s, 4 * A_V), BF16)],
        compiler_params=_params(("parallel", "parallel")),
        name="mla_prompt_attention",
    )(q, k, v)


def _run_phases(phases):
    prologue, segments, epilogue = phases
    prologue()
    for seg in segments:
        seg()
    epilogue()


def _hgrn_prompt_body(*refs, n_sb):
    _run_phases(_hgrn_phases(*refs, n_sb=n_sb))


def _hgrn_phases(q_ref, f_ref, i_ref, gt_ref, go_ref, o_ref, st_ref, st_scr, *, n_sb):
    t = pl.program_id(1)

    def prologue():
        @pl.when(t == 0)
        def _():
            st_scr[...] = jnp.zeros_like(st_scr)

    def epilogue():
        @pl.when(t == pl.num_programs(1) - 1)
        def _():
            for hd in range(B_HEADS):
                st_ref[0, hd] = st_scr[hd].T

    prepared = {}
    preps = [functools.partial(_hgrn_prep, q_ref, f_ref, prepared, sb=sb) for sb in range(n_sb)]
    applies = [functools.partial(_hgrn_apply, i_ref, gt_ref, go_ref, o_ref, st_scr, prepared, sb=sb)
               for sb in range(n_sb)]
    return prologue, preps + applies, epilogue


def _hgrn_level_masks():
    R = HGRN_BLOCK
    r_i = lax.broadcasted_iota(jnp.int32, (R, R), 0)
    c_i = lax.broadcasted_iota(jnp.int32, (R, R), 1)
    levels = []
    m = B_CHUNK
    while 2 * m <= R:
        sh = (2 * m).bit_length() - 1
        levels.append((m, ((r_i >> sh) == (c_i >> sh)) & ((r_i & (2 * m - 1)) >= m) & ((c_i & (2 * m - 1)) < m)))
        m *= 2
    return r_i, c_i, levels


def _hgrn_prep(q_ref, f_ref, prepared, *, sb):
    R = HGRN_BLOCK
    hb = B_HEADS * B_DK
    r_i, c_i, levels = _hgrn_level_masks()
    tri = jnp.where(c_i <= r_i, 1.0, 0.0).astype(BF16)

    rows = slice(sb * R, (sb + 1) * R)
    q = q_ref[0, rows, :]
    f = f_ref[0, rows, :]
    k = 1.0 - f
    lf = jnp.log(f)
    hi = lf.astype(BF16)
    r1 = lf - hi.astype(F32)
    mid = r1.astype(BF16)
    lo = (r1 - mid.astype(F32)).astype(BF16)
    G = _dot(tri, hi) + _dot(tri, mid) + _dot(tri, lo)
    g_last = G[R - 1:R, :]

    n_ch = R // B_CHUNK
    end_rows = [G[B_CHUNK * (j + 1) - 1:B_CHUNK * (j + 1), :] for j in range(n_ch)]
    g_end = jnp.concatenate(end_rows, axis=0)
    g_start = jnp.concatenate([jnp.zeros((1, hb), F32)] + end_rows[:-1], axis=0)

    def expand(per_chunk):
        return jnp.concatenate([jnp.broadcast_to(per_chunk[j:j + 1, :], (B_CHUNK, hb)) for j in range(n_ch)], axis=0)

    g0 = expand(g_start)
    q0 = q * jnp.exp(G - g0)
    k_end = k * jnp.exp(expand(g_end) - G)
    q_lv = [q0.astype(BF16)]
    k_lv = [(k * jnp.exp(g0 - G)).astype(BF16)]
    q_int = (q0 * expand(jnp.exp(g_start))).astype(BF16)
    k_st = (k_end * expand(jnp.exp(g_last - g_end))).astype(BF16)
    for mm, _ in levels:
        cpb = 2 * mm // B_CHUNK
        g_mid = jnp.concatenate([end_rows[(j // cpb) * cpb + cpb // 2 - 1] for j in range(n_ch)], axis=0)
        q_lv.append((q0 * expand(jnp.exp(jnp.minimum(g_start - g_mid, 0.0)))).astype(BF16))
        k_lv.append((k_end * expand(jnp.exp(jnp.minimum(g_mid - g_end, 0.0)))).astype(BF16))
    prepared[sb] = (q_lv, k_lv, q_int, k_st, jnp.exp(g_last))


def _hgrn_apply(i_ref, gt_ref, go_ref, o_ref, st_scr, prepared, *, sb):
    R = HGRN_BLOCK
    r_i, c_i, levels = _hgrn_level_masks()
    base_mask = ((r_i >> 4) == (c_i >> 4)) & (c_i <= r_i)
    go = go_ref[...]
    rows = slice(sb * R, (sb + 1) * R)
    v = i_ref[0, rows, :]
    q_lv, k_lv, q_int, k_st, decay = prepared[sb]

    heads = [slice(hd * B_DK, (hd + 1) * B_DK) for hd in range(B_HEADS)]
    inter, state_in, atts = [], [], []
    for hd, hs in enumerate(heads):
        st = st_scr[hd]
        inter.append(_dot_nt(q_int[:, hs], st.astype(BF16)))
        state_in.append(st * decay[:, hs] + _dot_tn(v[:, hs], k_st[:, hs]))
    for hs in heads:
        att = jnp.where(base_mask, _dot_nt(q_lv[0][:, hs], k_lv[0][:, hs]), 0.0)
        for li, (_, msk) in enumerate(levels):
            att = jnp.where(msk, _dot_nt(q_lv[li + 1][:, hs], k_lv[li + 1][:, hs]), att)
        atts.append(att.astype(BF16))
    for hd, hs in enumerate(heads):
        st_scr[hd] = state_in[hd]
        o = _dot(atts[hd], v[:, hs]) + inter[hd]
        o = o * lax.rsqrt(jnp.mean(o * o, axis=-1, keepdims=True) + RMS_EPS) * go
        o_ref[0, rows, hs] = (o * gt_ref[0, rows, hs]).astype(BF16)


def _hgrn_prompt(hq, hf, hi, hg, g_out, rows_per_step):
    b, s, hb = hq.shape
    assert s % HGRN_BLOCK == 0
    ts = rows_per_step
    blk = lambda i, t: (i, t, 0)
    return pl.pallas_call(
        functools.partial(_hgrn_prompt_body, n_sb=ts // HGRN_BLOCK),
        grid=(b, s // ts),
        in_specs=[pl.BlockSpec((1, ts, hb), blk)] * 4 + [pl.BlockSpec((1, B_DV), lambda i, t: (0, 0))],
        out_specs=[
            pl.BlockSpec((1, ts, hb), blk),
            pl.BlockSpec((1, B_HEADS, B_DK, B_DV), lambda i, t: (i, 0, 0, 0)),
        ],
        out_shape=[
            jax.ShapeDtypeStruct((b, s, hb), BF16),
            jax.ShapeDtypeStruct((b, B_HEADS, B_DK, B_DV), F32),
        ],
        scratch_shapes=[pltpu.VMEM((B_HEADS, B_DV, B_DK), F32)],
        compiler_params=_params(("parallel", "arbitrary")),
        name="hgrn2_prompt",
    )(hq, hf, hi, hg, g_out.reshape(1, B_DV))


def _hgrn_step_body(q_ref, f_ref, i_ref, gt_ref, go_ref, s_ref, o_ref, so_ref, *, nb):
    go = go_ref[...]
    pad = jnp.zeros((B_DK - 3 * nb, B_DK), F32)
    for hd in range(B_HEADS):
        hs = slice(hd * B_DK, (hd + 1) * B_DK)
        f = f_ref[:, hs]
        cols = jnp.concatenate([q_ref[:, hs], f, 1.0 - f, pad], axis=0).T
        for j in range(nb):
            s_new = cols[:, nb + j:nb + j + 1] * s_ref[j, hd] + cols[:, 2 * nb + j:2 * nb + j + 1] * i_ref[j:j + 1, hs].astype(F32)
            so_ref[j, hd] = s_new
            o = jnp.sum(cols[:, j:j + 1] * s_new, axis=0, keepdims=True)
            o = o * lax.rsqrt(jnp.mean(o * o, axis=-1, keepdims=True) + RMS_EPS) * go
            o_ref[j:j + 1, hs] = (o * gt_ref[j:j + 1, hs]).astype(BF16)


def _hgrn_step(hq, hf, hi, hg, g_out, state, layer, nb=8):
    n, hb = hq.shape
    nb = _row_tile(n, nb)
    row = lambda i: (i, 0)
    st_in = lambda i: (layer, i, 0, 0, 0)
    st_out = lambda i: (i, 0, 0, 0)
    return pl.pallas_call(
        functools.partial(_hgrn_step_body, nb=nb),
        grid=(n // nb,),
        in_specs=[pl.BlockSpec((nb, hb), row)] * 4 + [_const_spec((1, B_DV)), pl.BlockSpec((None, nb, B_HEADS, B_DK, B_DV), st_in)],
        out_specs=[pl.BlockSpec((nb, hb), row), pl.BlockSpec((nb, B_HEADS, B_DK, B_DV), st_out)],
        out_shape=[jax.ShapeDtypeStruct((n, hb), BF16), jax.ShapeDtypeStruct(state.shape[1:], F32)],
        compiler_params=_params(("parallel",)),
        name="hgrn2_step",
    )(hq, hf, hi, hg, g_out.reshape(1, B_DV), state)


def _small_matmul_body(a_ref, w_ref, o_ref):
    o_ref[...] = _dot(a_ref[...].astype(BF16), w_ref[...]).astype(o_ref.dtype)


def _small_matmul(a, w, out_dtype):
    n = a.shape[0]
    out_spec = pl.BlockSpec((n, w.shape[1]), lambda i: (0, 0))
    return pl.pallas_call(
        _small_matmul_body,
        grid=(1,),
        in_specs=[_const_spec(a.shape), _const_spec(w.shape)],
        out_specs=out_spec,
        out_shape=jax.ShapeDtypeStruct((n, w.shape[1]), out_dtype),
        compiler_params=_params(("arbitrary",)),
        name="sample_matmul",
    )(a, w)


def _page_copies(pt_ref, ckv_hbm, kpe_hbm, ckv_buf, kpe_buf, sems, step, slot, *, n_chunk, pc, page):
    b = step // n_chunk
    c = step % n_chunk
    copies = []
    for p in range(pc):
        pg = pt_ref[b, c * pc + p]
        copies.append(pltpu.make_async_copy(ckv_hbm.at[pg], ckv_buf.at[slot, pl.ds(p * page, page), :], sems.at[slot, 0]))
        copies.append(pltpu.make_async_copy(kpe_hbm.at[pg], kpe_buf.at[slot, :, pl.ds(p * page, page)], sems.at[slot, 1]))
    return copies


def _attn_decode_body(pt_ref, *refs, **kw):
    _run_phases(_decode_phases(pl.program_id(0), pl.num_programs(0), pt_ref, *refs, **kw))


def _decode_phases(s, total, pt_ref, ql_ref, qp_ref, cn_ref, kn_ref, ckv_hbm, kpe_hbm, o_ref,
                   ckv_buf, kpe_buf, sems, m_scr, l_scr, acc_scr, *, layer, n_chunk, pc, page):
    slot = s % 2
    c = s % n_chunk
    copies = functools.partial(_page_copies, pt_ref, ckv_hbm.at[layer], kpe_hbm.at[layer], ckv_buf, kpe_buf, sems,
                               n_chunk=n_chunk, pc=pc, page=page)

    def prologue():
        @pl.when(s == 0)
        def _():
            for cp in copies(s, slot):
                cp.start()

        @pl.when(s + 1 < total)
        def _():
            for cp in copies(s + 1, 1 - slot):
                cp.start()

        for cp in copies(s, slot):
            cp.wait()

        @pl.when(c == 0)
        def _():
            cn = cn_ref[0]
            s_self = (jnp.sum(ql_ref[0] * cn, axis=-1, keepdims=True)
                      + jnp.sum(qp_ref[0] * kn_ref[0], axis=-1, keepdims=True))
            m_scr[...] = s_self
            l_scr[...] = jnp.ones_like(l_scr)
            acc_scr[...] = jnp.broadcast_to(cn, acc_scr.shape)

    n_keys = pc * page
    part = n_keys // DECODE_KEY_PARTS if n_keys % (DECODE_KEY_PARTS * LANES) == 0 else n_keys
    n_parts = n_keys // part
    scores, stats = {}, {}

    def part_scores(i):
        ks = slice(i * part, (i + 1) * part)
        scores[i] = _dot_nt(ql_ref[0], ckv_buf[slot, ks, :]) + _dot(qp_ref[0], kpe_buf[slot, :, ks])

    def part_pv(i):
        ks = slice(i * part, (i + 1) * part)
        m_i = jnp.max(scores[i], axis=-1, keepdims=True)
        p = jnp.exp2(scores[i] - m_i)
        stats[i] = (m_i, jnp.sum(p, axis=-1, keepdims=True), _dot(p, ckv_buf[slot, ks, :]))

    def merge():
        m_old = m_scr[...]
        m_new = m_old
        for i in range(n_parts):
            m_new = jnp.maximum(m_new, stats[i][0])
        alpha = jnp.exp2(m_old - m_new)
        l_new = alpha * l_scr[...]
        acc_new = alpha * acc_scr[...]
        for i in range(n_parts):
            m_i, l_i, acc_i = stats[i]
            w_i = jnp.exp2(m_i - m_new)
            l_new = l_new + w_i * l_i
            acc_new = acc_new + w_i * acc_i
        l_scr[...] = l_new
        acc_scr[...] = acc_new
        m_scr[...] = m_new

    def epilogue():
        @pl.when(c == n_chunk - 1)
        def _():
            o_ref[0] = acc_scr[...] / l_scr[...]

    segments = ([functools.partial(part_scores, i) for i in range(n_parts)]
                + [functools.partial(part_pv, i) for i in range(n_parts)] + [merge])
    return prologue, segments, epilogue


def _decode_pages_per_step(n_pages, page):
    budget = DECODE_BUFFER_BYTES // (2 * page * (A_KV_LORA + A_ROPE) * 4)
    return max(p for p in range(1, n_pages + 1) if n_pages % p == 0 and p <= max(budget, 1))


def _attn_decode(page_table, q_lat, q_pe, ckv_new, kpe_new, pool_ckv, pool_kpe_t, layer):
    db, n_pages = page_table.shape
    page = pool_ckv.shape[2]
    pc = _decode_pages_per_step(n_pages, page)
    n_chunk = n_pages // pc
    per_row = lambda s, pt: (s // n_chunk, 0, 0)
    grid_spec = pltpu.PrefetchScalarGridSpec(
        num_scalar_prefetch=1,
        grid=(db * n_chunk,),
        in_specs=[
            pl.BlockSpec((1, A_HEADS, A_KV_LORA), per_row), pl.BlockSpec((1, A_HEADS, A_ROPE), per_row),
            pl.BlockSpec((1, 1, A_KV_LORA), per_row), pl.BlockSpec((1, 1, A_ROPE), per_row),
            pl.BlockSpec(memory_space=pl.ANY), pl.BlockSpec(memory_space=pl.ANY),
        ],
        out_specs=pl.BlockSpec((1, A_HEADS, A_KV_LORA), per_row),
        scratch_shapes=_decode_scratch(pc, page),
    )
    return pl.pallas_call(
        functools.partial(_attn_decode_body, layer=layer, n_chunk=n_chunk, pc=pc, page=page),
        grid_spec=grid_spec,
        out_shape=jax.ShapeDtypeStruct((db, A_HEADS, A_KV_LORA), F32),
        compiler_params=_params(("arbitrary",)),
        name="mla_paged_decode",
    )(page_table, q_lat, q_pe, ckv_new, kpe_new, pool_ckv, pool_kpe_t)


def _decode_scratch(pc, page):
    return [
        pltpu.VMEM((2, pc * page, A_KV_LORA), F32), pltpu.VMEM((2, A_ROPE, pc * page), F32),
        pltpu.SemaphoreType.DMA((2, 2)),
        pltpu.VMEM((A_HEADS, 1), F32), pltpu.VMEM((A_HEADS, 1), F32), pltpu.VMEM((A_HEADS, A_KV_LORA), F32),
    ]


def _hgrn_decode_body(pt_ref, q_ref, f_ref, i_ref, gt_ref, go_ref, ql_ref, qp_ref, cn_ref, kn_ref, ckv_hbm, kpe_hbm,
                      o_ref, st_ref, ol_ref, st_scr, ckv_buf, kpe_buf, sems, m_scr, l_scr, acc_scr, *, n_sb, decode_kw):
    step = pl.program_id(0) * pl.num_programs(1) + pl.program_id(1)
    total = pl.num_programs(0) * pl.num_programs(1)
    dec_pro, dec_segs, dec_epi = _decode_phases(step, total, pt_ref, ql_ref, qp_ref, cn_ref, kn_ref, ckv_hbm, kpe_hbm,
                                                 ol_ref, ckv_buf, kpe_buf, sems, m_scr, l_scr, acc_scr, **decode_kw)
    hgr_pro, hgr_segs, hgr_epi = _hgrn_phases(q_ref, f_ref, i_ref, gt_ref, go_ref, o_ref, st_ref, st_scr, n_sb=n_sb)
    dec_pro()
    hgr_pro()
    n_prep = len(hgr_segs) // 2
    n_score = (len(dec_segs) - 1) // 2
    order = (dec_segs[:1] + hgr_segs[:n_prep] + dec_segs[1:n_score]
             + _interleave(dec_segs[n_score:], hgr_segs[n_prep:]))
    for seg in order:
        seg()
    dec_epi()
    hgr_epi()


def _interleave(a, b):
    out, j = [], 0
    for i, seg in enumerate(a):
        out.append(seg)
        upto = (i + 1) * len(b) // len(a)
        out.extend(b[j:upto])
        j = upto
    return out


def _merged_chunks(n_steps, db, n_pages, page):
    if n_steps % db:
        return None
    n_chunk = n_steps // db
    if n_pages % n_chunk or n_pages // n_chunk > _decode_pages_per_step(n_pages, page):
        return None
    return n_chunk


def _hgrn_prompt_with_decode(hq, hf, hi, hg, g_out, page_table, q_lat, q_pe, ckv_new, kpe_new, pool_ckv, pool_kpe_t,
                             layer, n_chunk, rows_per_step):
    b, s, hb = hq.shape
    ts = rows_per_step
    nt = s // ts
    db, n_pages = page_table.shape
    page = pool_ckv.shape[2]
    pc = n_pages // n_chunk
    blk = lambda i, t, pt: (i, t, 0)
    per_row = lambda i, t, pt: ((i * nt + t) // n_chunk, 0, 0)
    grid_spec = pltpu.PrefetchScalarGridSpec(
        num_scalar_prefetch=1,
        grid=(b, nt),
        in_specs=[pl.BlockSpec((1, ts, hb), blk)] * 4 + [
            pl.BlockSpec((1, B_DV), lambda i, t, pt: (0, 0)),
            pl.BlockSpec((1, A_HEADS, A_KV_LORA), per_row), pl.BlockSpec((1, A_HEADS, A_ROPE), per_row),
            pl.BlockSpec((1, 1, A_KV_LORA), per_row), pl.BlockSpec((1, 1, A_ROPE), per_row),
            pl.BlockSpec(memory_space=pl.ANY), pl.BlockSpec(memory_space=pl.ANY),
        ],
        out_specs=[
            pl.BlockSpec((1, ts, hb), blk),
            pl.BlockSpec((1, B_HEADS, B_DK, B_DV), lambda i, t, pt: (i, 0, 0, 0)),
            pl.BlockSpec((1, A_HEADS, A_KV_LORA), per_row),
        ],
        scratch_shapes=[pltpu.VMEM((B_HEADS, B_DV, B_DK), F32)] + _decode_scratch(pc, page),
    )
    decode_kw = dict(layer=layer, n_chunk=n_chunk, pc=pc, page=page)
    return pl.pallas_call(
        functools.partial(_hgrn_decode_body, n_sb=ts // HGRN_BLOCK, decode_kw=decode_kw),
        grid_spec=grid_spec,
        out_shape=[
            jax.ShapeDtypeStruct((b, s, hb), BF16),
            jax.ShapeDtypeStruct((b, B_HEADS, B_DK, B_DV), F32),
            jax.ShapeDtypeStruct((db, A_HEADS, A_KV_LORA), F32),
        ],
        compiler_params=_params(("arbitrary", "arbitrary")),
        name="hgrn2_prompt_with_decode",
    )(page_table, hq, hf, hi, hg, g_out.reshape(1, B_DV), q_lat, q_pe, ckv_new, kpe_new, pool_ckv, pool_kpe_t)


def _cmix_prompt_body(x_ref, g_ref, win_ref, wc_ref, wout_ref, o_ref, buf_ref, carry, *, n_tiles):
    t = pl.program_id(1)
    tm = x_ref.shape[1]
    cw = wout_ref.shape[0]

    @pl.when(t == 0)
    def _():
        carry[...] = jnp.zeros_like(carry)

    x = x_ref[0]
    h = _rms(x, g_ref[...]).astype(BF16)
    p = _dot(h, win_ref[...])
    v = p[:, cw:2 * cw] * p[:, 2 * cw:]
    row = lax.broadcasted_iota(jnp.int32, v.shape, 0)
    prev1 = jnp.broadcast_to(carry[1:2, :], v.shape)
    prev2 = jnp.broadcast_to(carry[0:1, :], v.shape)
    v1 = jnp.where(row == 0, prev1, pltpu.roll(v, 1, 0))
    v2 = jnp.where(row == 0, prev2, jnp.where(row == 1, prev1, pltpu.roll(v, 2, 0)))
    y = wc_ref[0:1, :] * v2 + wc_ref[1:2, :] * v1 + wc_ref[2:3, :] * v
    o_ref[0] = x + _dot((p[:, :cw] * y).astype(BF16), wout_ref[...])
    carry[0:2, :] = v[tm - 2:, :]

    @pl.when(t == n_tiles - 1)
    def _():
        buf_ref[0] = v[tm - 2:, :]


def _cmix_prompt(x, g, w_in, w_conv, w_out, tm=1024):
    b, s, d = x.shape
    tm = _row_tile(s, tm)
    n_tiles = s // tm
    cw = w_out.shape[0]
    blk = lambda i, t: (i, t, 0)
    cst = lambda shape: pl.BlockSpec(shape, lambda i, t: (0,) * len(shape), pipeline_mode=pl.Buffered(1))
    return pl.pallas_call(
        functools.partial(_cmix_prompt_body, n_tiles=n_tiles),
        grid=(b, n_tiles),
        in_specs=[pl.BlockSpec((1, tm, d), blk), cst((1, d)), cst(w_in.shape), cst(w_conv.shape), cst(w_out.shape)],
        out_specs=[pl.BlockSpec((1, tm, d), blk), pl.BlockSpec((1, C_KERNEL - 1, cw), lambda i, t: (i, 0, 0))],
        out_shape=[jax.ShapeDtypeStruct((b, s, d), F32), jax.ShapeDtypeStruct((b, C_KERNEL - 1, cw), F32)],
        scratch_shapes=[pltpu.VMEM((8, cw), F32)],
        compiler_params=_params(("parallel", "arbitrary")),
        name="conv_mixer_prompt",
    )(x, g.reshape(1, d), w_in, w_conv, w_out)


def _cmix_step_body(x_ref, b0_ref, b1_ref, g_ref, win_ref, wc_ref, wout_ref, o_ref, v_ref):
    cw = wout_ref.shape[0]
    x = x_ref[...]
    h = _rms(x, g_ref[...]).astype(BF16)
    p = _dot(h, win_ref[...])
    v = p[:, cw:2 * cw] * p[:, 2 * cw:]
    y = wc_ref[0:1, :] * b0_ref[...] + wc_ref[1:2, :] * b1_ref[...] + wc_ref[2:3, :] * v
    o_ref[...] = x + _dot((p[:, :cw] * y).astype(BF16), wout_ref[...])
    v_ref[...] = v


def _cmix_step(x, buf0, buf1, g, w_in, w_conv, w_out):
    n, d = x.shape
    cw = w_out.shape[0]
    args = (x, buf0, buf1, g.reshape(1, d), w_in, w_conv, w_out)
    return pl.pallas_call(
        _cmix_step_body,
        grid=(1,),
        in_specs=[_const_spec(a.shape) for a in args],
        out_specs=[pl.BlockSpec((n, d), lambda i: (0, 0)), pl.BlockSpec((n, cw), lambda i: (0, 0))],
        out_shape=[jax.ShapeDtypeStruct((n, d), F32), jax.ShapeDtypeStruct((n, cw), F32)],
        compiler_params=_params(("arbitrary",)),
        name="conv_mixer_step",
    )(*args)


def _prep_ab_weights(w_in, g_q, g_kv, w_uq, w_ukv, w_out):
    d = w_in.shape[0]
    o_kv = A_Q_LORA
    o_pe = o_kv + A_KV_LORA
    o_h = o_pe + A_ROPE
    zn = jnp.zeros((d, A_NOPE), F32)
    zt = jnp.zeros((d, HEAD_SLOT - A_NOPE - A_ROPE), F32)
    w_in_p = jnp.concatenate([w_in[:, :o_pe], zn, w_in[:, o_pe:o_h], zt, w_in[:, o_h:]], axis=1).astype(BF16)

    wq = w_uq.reshape(A_Q_LORA, A_HEADS, A_NOPE + A_ROPE)
    zq = jnp.zeros((A_Q_LORA, A_HEADS, HEAD_SLOT - A_NOPE - A_ROPE), F32)
    w_q = jnp.concatenate([wq, zq], axis=-1).reshape(A_Q_LORA, A_HEADS * HEAD_SLOT).astype(BF16)

    wkv = w_ukv.reshape(A_KV_LORA, A_HEADS, A_NOPE + A_V)
    w_uk, w_uv = wkv[..., :A_NOPE], wkv[..., A_NOPE:]
    k_pad = jnp.concatenate([w_uk, jnp.zeros((A_KV_LORA, A_HEADS, HEAD_SLOT - A_NOPE), F32)], axis=-1)
    w_kv = jnp.concatenate(
        [k_pad.reshape(A_KV_LORA, A_HEADS * HEAD_SLOT), w_uv.reshape(A_KV_LORA, A_HEADS * A_V)], axis=1).astype(BF16)

    eye = jnp.eye(A_HEADS, dtype=F32)
    uk_t = jnp.transpose(w_uk, (1, 2, 0))
    uk_t = jnp.concatenate([uk_t, jnp.zeros((A_HEADS, HEAD_SLOT - A_NOPE, A_KV_LORA), F32)], axis=1)
    absorb = (uk_t[:, :, None, :] * eye[:, None, :, None]).reshape(A_HEADS * HEAD_SLOT, A_HEADS * A_KV_LORA)
    sel = jnp.zeros((HEAD_SLOT, A_ROPE), F32).at[A_NOPE + jnp.arange(A_ROPE), jnp.arange(A_ROPE)].set(1.0)
    sel = (sel[None, :, None, :] * eye[:, None, :, None]).reshape(A_HEADS * HEAD_SLOT, A_HEADS * A_ROPE)
    w_dec_q = jnp.concatenate([absorb, sel], axis=1).astype(BF16)
    uv_t = jnp.transpose(w_uv, (1, 0, 2))
    w_dec_o = (uv_t[:, :, None, :] * eye[:, None, :, None]).reshape(A_HEADS * A_KV_LORA, A_HEADS * A_V).astype(BF16)

    n_a = A_HEADS * A_V
    return dict(w_in=w_in_p, g_q=g_q.reshape(1, -1), g_kv=g_kv.reshape(1, -1), w_q=w_q, w_kv=w_kv,
                w_dec_q=w_dec_q, w_dec_o=w_dec_o, w_out_a=w_out[:n_a].astype(BF16), w_out_b=w_out[n_a:].astype(BF16))


def _rope_tables(pos):
    inv = ROPE_BASE ** (-jnp.arange(0, A_ROPE, 2, dtype=F32) / A_ROPE)
    ang = pos.astype(F32)[:, None] * inv[None, :]
    cos, sin = jnp.cos(ang), jnp.sin(ang)
    n = pos.shape[0]
    zero = jnp.zeros_like(sin)
    head = jnp.zeros((n, A_NOPE), F32)
    tail = jnp.zeros((n, HEAD_SLOT - A_NOPE - A_ROPE), F32)
    cos_t = jnp.concatenate([jnp.ones((n, A_NOPE), F32), cos, cos, tail], axis=1)
    sin_lo = jnp.concatenate([head, -sin, zero, tail], axis=1)
    sin_hi = jnp.concatenate([head, zero, sin, tail], axis=1)
    c = SM_SCALE * LOG2_E
    return cos_t * c, sin_lo * c, sin_hi * c, cos_t, sin_lo, sin_hi


def kernel(x_prompt, x_sample, cache_ckv, cache_kpe, state_hgrn, state_conv, page_table, norm_g, final_norm_g,
           w_ffn_gate, w_ffn_up, w_ffn_down, w_in_ab, g_q_lora, g_kv_lora, w_uq, w_ukv, hgrn_lower_bound,
           g_hgrn_out, w_out_ab, w_in_c, w_conv_c, w_out_c):
    b_p, s_p, d = x_prompt.shape
    d_b, s_d, _ = x_sample.shape
    assert s_d == 1
    depth = norm_g.shape[0]
    past_len = page_table.shape[1] * cache_ckv.shape[2]
    n_p = b_p * s_p

    tab_p = _rope_tables(jnp.arange(s_p))
    tab_d = _rope_tables(jnp.full((d_b,), past_len))
    pool_kpe_t = jnp.swapaxes(cache_kpe, 2, 3)

    w_ffn = (w_ffn_gate.astype(BF16), w_ffn_up.astype(BF16), w_ffn_down.astype(BF16))

    xp = x_prompt.reshape(n_p, d)
    xd = x_sample.reshape(d_b, d)
    outs = {k: [] for k in ("ckv_p", "kpe_p", "ckv_d", "kpe_d", "hg_p", "hg_d", "cv_p", "cv_d")}

    for layer in range(depth):
        j = layer // 2
        last = layer == depth - 1
        xp = _ffn(xp, norm_g[layer, 0], w_ffn, (layer, 0))
        xd = _ffn(xd, norm_g[layer, 0], w_ffn, (layer, 0))
        ffn_b = (norm_g[layer, 2], w_ffn, (layer, 1))
        fin = final_norm_g if last else None
        if layer % 2 == 0:
            w = _prep_ab_weights(w_in_ab[j], g_q_lora[j], g_kv_lora[j], w_uq[j], w_ukv[j], w_out_ab[j])
            ckv, kpe_t, q, k, v, hq, hf, hi, hg = _ab_pre(xp, norm_g[layer, 1], w, tab_p, hgrn_lower_bound, layer)
            outs["ckv_p"].append(ckv.reshape(b_p, s_p, -1))
            outs["kpe_p"].append(jnp.swapaxes(kpe_t, 1, 2))
            o_a = _attn_prompt(q.reshape(b_p, s_p, -1), k.reshape(b_p, s_p, -1), v.reshape(b_p, s_p, -1))
            sh = (b_p, s_p, -1)
            hgrn_in = (hq.reshape(sh), hf.reshape(sh), hi.reshape(sh), hg.reshape(sh), g_hgrn_out[j])
            ckv, kpe_t, q, _, _, hq, hf, hi, hg = _ab_pre(xd, norm_g[layer, 1], w, tab_d, hgrn_lower_bound, layer)
            kpe = kpe_t[0].T
            qd = _small_matmul(q, w["w_dec_q"], F32)
            n_lat = A_HEADS * A_KV_LORA
            dec_in = (page_table, qd[:, :n_lat].reshape(d_b, A_HEADS, A_KV_LORA), qd[:, n_lat:].reshape(d_b, A_HEADS, A_ROPE),
                      ckv.reshape(d_b, 1, -1), kpe.reshape(d_b, 1, -1), cache_ckv, pool_kpe_t, j)
            ts = _row_tile(s_p, HGRN_ROWS_PER_STEP)
            n_chunk = _merged_chunks(b_p * (s_p // ts), d_b, page_table.shape[1], cache_ckv.shape[2])
            if n_chunk is None:
                o_b_p, st_p = _hgrn_prompt(*hgrn_in, rows_per_step=ts)
                o_lat = _attn_decode(*dec_in)
            else:
                o_b_p, st_p, o_lat = _hgrn_prompt_with_decode(*hgrn_in, *dec_in, n_chunk, ts)
            outs["hg_p"].append(st_p)
            pre = ((o_a.reshape(n_p, -1), w["w_out_a"]), (o_b_p.reshape(n_p, -1), w["w_out_b"]))
            xp = _ffn(xp, *ffn_b, pre=pre, final_g=fin)
            o_a = _small_matmul(o_lat.reshape(d_b, n_lat), w["w_dec_o"], BF16)
            o_b, st_d = _hgrn_step(hq, hf, hi, hg, g_hgrn_out[j], state_hgrn, j)
            xd = _ffn(xd, *ffn_b, pre=((o_a, w["w_out_a"]), (o_b, w["w_out_b"])), final_g=fin)
            outs["ckv_d"].append(ckv)
            outs["kpe_d"].append(kpe)
            outs["hg_d"].append(st_d)
        else:
            win = w_in_c[j].astype(BF16)
            wout = w_out_c[j].astype(BF16)
            xp3, buf_p = _cmix_prompt(xp.reshape(b_p, s_p, d), norm_g[layer, 1], win, w_conv_c[j], wout)
            xp = _ffn(xp3.reshape(n_p, d), *ffn_b, final_g=fin)
            xd, v_new = _cmix_step(xd, state_conv[j, :, 0], state_conv[j, :, 1], norm_g[layer, 1], win, w_conv_c[j], wout)
            xd = _ffn(xd, *ffn_b, final_g=fin)
            outs["cv_p"].append(buf_p)
            outs["cv_d"].append(jnp.stack([state_conv[j, :, 1], v_new], axis=1))

    return (xp.reshape(b_p, s_p, d), xd.reshape(d_b, s_d, d),
            jnp.stack(outs["ckv_p"]), jnp.stack(outs["kpe_p"]),
            jnp.stack(outs["ckv_d"]).reshape(-1, d_b, s_d, A_KV_LORA), jnp.stack(outs["kpe_d"]).reshape(-1, d_b, s_d, A_ROPE),
            jnp.stack(outs["hg_p"]), jnp.stack(outs["hg_d"]), jnp.stack(outs["cv_p"]), jnp.stack(outs["cv_d"]))
```

```python
import functools

import jax
import jax.numpy as jnp
from jax import lax
from jax.experimental import pallas as pl
from jax.experimental.pallas import tpu as pltpu

F32 = jnp.float32
BF16 = jnp.bfloat16

A_HEADS = 8
A_NOPE = 64
A_ROPE = 32
A_V = 64
A_Q_LORA = 384
A_KV_LORA = 256
ROPE_BASE = 10000.0
SM_SCALE = (A_NOPE + A_ROPE) ** -0.5
LOG2_E = 1.4426950408889634
B_HEADS = 4
B_DK = 128
B_DV = 128
B_CHUNK = 16
C_KERNEL = 3
RMS_EPS = 1e-6

LANES = 128
HEAD_SLOT = LANES
HGRN_BLOCK = 128
HGRN_ROWS_PER_STEP = 512
ATTN_HEAD_PAIRS = 4
ATTN_SCORES_AHEAD = 1
VMEM_LIMIT = 56 * 1024 * 1024
DECODE_BUFFER_BYTES = 40 * 1024 * 1024
DECODE_KEY_PARTS = 2


def _params(sem, vmem=VMEM_LIMIT):
    return pltpu.CompilerParams(dimension_semantics=sem, vmem_limit_bytes=vmem)


def _const_spec(shape):
    nd = len(shape)
    return pl.BlockSpec(shape, lambda *_: (0,) * nd, pipeline_mode=pl.Buffered(1))


def _rms(x, g):
    return x * lax.rsqrt(jnp.mean(x * x, axis=-1, keepdims=True) + RMS_EPS) * g


def _dot(a, b):
    return jnp.dot(a, b, preferred_element_type=F32)


def _dot_nt(a, b):
    return lax.dot_general(a, b, (((1,), (1,)), ((), ())), preferred_element_type=F32)


def _dot_tn(a, b):
    return lax.dot_general(a, b, (((0,), (0,)), ((), ())), preferred_element_type=F32)


def _silu(x):
    return x * jax.nn.sigmoid(x)


def _row_tile(n, want):
    t = min(n, want)
    assert n % t == 0, (n, t)
    return t


def _ffn_body(*refs, n_chunks, tf, n_pre, has_final):
    x_ref = refs[0]
    pre = refs[1:1 + 2 * n_pre]
    g_ref, wg_ref, wu_ref, wd_ref = refs[1 + 2 * n_pre:5 + 2 * n_pre]
    rest = refs[5 + 2 * n_pre:]
    gf_ref = rest[0] if has_final else None
    o_ref = rest[-1]

    x = x_ref[...]
    for p in range(n_pre):
        x = x + _dot(pre[2 * p][...], pre[2 * p + 1][...])
    h = _rms(x, g_ref[...]).astype(BF16)
    acc = None
    for c in range(n_chunks):
        cs = slice(c * tf, (c + 1) * tf)
        gate = _dot(h, wg_ref[:, cs])
        up = _dot(h, wu_ref[:, cs])
        a = (_silu(gate) * up).astype(BF16)
        part = _dot(a, wd_ref[cs, :])
        acc = part if acc is None else acc + part
    y = x + 0.5 * acc
    if has_final:
        y = _rms(y, gf_ref[...])
    o_ref[...] = y


def _ffn(x, g, weights, which, pre=(), final_g=None, tm=1024):
    n, d = x.shape
    f = weights[0].shape[-1]
    tf = 256 if f % 256 == 0 else LANES
    tm = _row_tile(n, tm)
    row = lambda i: (i, 0)
    in_specs = [pl.BlockSpec((tm, d), row)]
    args = [x]
    for act, w in pre:
        in_specs += [pl.BlockSpec((tm, act.shape[1]), row), _const_spec(w.shape)]
        args += [act, w]
    pick = lambda i: (*which, 0, 0)
    w_specs = [pl.BlockSpec((None, None) + w.shape[2:], pick, pipeline_mode=pl.Buffered(1)) for w in weights]
    in_specs += [_const_spec((1, d))] + w_specs
    args += [g.reshape(1, d), *weights]
    if final_g is not None:
        in_specs.append(_const_spec((1, d)))
        args.append(final_g.reshape(1, d))
    body = functools.partial(_ffn_body, n_chunks=f // tf, tf=tf, n_pre=len(pre), has_final=final_g is not None)
    return pl.pallas_call(
        body,
        grid=(n // tm,),
        in_specs=in_specs,
        out_specs=pl.BlockSpec((tm, d), row),
        out_shape=jax.ShapeDtypeStruct((n, d), F32),
        compiler_params=_params(("parallel",)),
        name="half_ffn",
    )(*args)


def _rope_slot(x, cos, sin_lo, sin_hi):
    half = A_ROPE // 2
    return x * cos + pltpu.roll(x, HEAD_SLOT - half, 1) * sin_lo + pltpu.roll(x, half, 1) * sin_hi


def _ab_pre_body(x_ref, g_ref, win_ref, gq_ref, gkv_ref, wq_ref, wkv_ref, cq_ref, sq_lo_ref, sq_hi_ref,
                 c_ref, s_lo_ref, s_hi_ref, lb_ref,
                 ckv_ref, kpet_ref, q_ref, k_ref, v_ref, hq_ref, hf_ref, hi_ref, hg_ref, *, layer):
    hb = B_HEADS * B_DK
    hw = A_HEADS * HEAD_SLOT
    o_kv = A_Q_LORA
    o_pe = o_kv + A_KV_LORA
    o_h = o_pe + HEAD_SLOT
    h = _rms(x_ref[...], g_ref[...]).astype(BF16)
    proj = _dot(h, win_ref[...])
    q_tabs = (cq_ref[...], sq_lo_ref[...], sq_hi_ref[...])

    cq = _rms(proj[:, :A_Q_LORA], gq_ref[...]).astype(BF16)
    qq = _dot(cq, wq_ref[...])
    for hd in range(A_HEADS):
        sl = slice(hd * HEAD_SLOT, (hd + 1) * HEAD_SLOT)
        q_ref[:, sl] = _rope_slot(qq[:, sl], *q_tabs).astype(BF16)

    ckv = _rms(proj[:, o_kv:o_pe], gkv_ref[...])
    ckv_ref[...] = ckv
    kpe = _rope_slot(proj[:, o_pe:o_h], c_ref[...], s_lo_ref[...], s_hi_ref[...])
    kpet_ref[0] = kpe.T[A_NOPE:A_NOPE + A_ROPE, :]
    kv = _dot(ckv.astype(BF16), wkv_ref[...])
    for hd in range(A_HEADS):
        sl = slice(hd * HEAD_SLOT, (hd + 1) * HEAD_SLOT)
        k_ref[:, sl] = (kv[:, sl] + kpe).astype(BF16)
    v_ref[...] = kv[:, hw:].astype(BF16)

    lbr = lb_ref[...]
    e = jnp.exp(lbr - jnp.max(lbr, axis=0, keepdims=True))
    lb = jnp.sum(e[:layer + 1], axis=0, keepdims=True) / jnp.sum(e, axis=0, keepdims=True)
    hq_ref[...] = (_silu(proj[:, o_h:o_h + hb]) * (B_DK ** -0.5)).astype(BF16)
    hf_ref[...] = lb + (1.0 - lb) * jax.nn.sigmoid(proj[:, o_h + hb:o_h + 2 * hb])
    hi_ref[...] = proj[:, o_h + 2 * hb:o_h + 3 * hb].astype(BF16)
    hg_ref[...] = _silu(proj[:, o_h + 3 * hb:o_h + 4 * hb]).astype(BF16)


def _ab_pre(x, g, w, tables, lb_raw, layer, tm=1024):
    n, d = x.shape
    s_len = tables[0].shape[0]
    tm = _row_tile(s_len, tm)
    assert n % s_len == 0
    n_pos_tiles = s_len // tm
    row = lambda i: (i, 0)
    pos = lambda i: (i % n_pos_tiles, 0)
    hw = A_HEADS * HEAD_SLOT
    hb = B_HEADS * B_DK
    outs = [
        (A_KV_LORA, F32), None, (hw, BF16), (hw, BF16), (A_HEADS * A_V, BF16),
        (hb, BF16), (hb, F32), (hb, BF16), (hb, BF16),
    ]
    kpet_spec = pl.BlockSpec((1, A_ROPE, tm), lambda i: (i // n_pos_tiles, 0, i % n_pos_tiles))
    kpet_shape = jax.ShapeDtypeStruct((n // s_len, A_ROPE, s_len), F32)
    return pl.pallas_call(
        functools.partial(_ab_pre_body, layer=layer),
        grid=(n // tm,),
        in_specs=[
            pl.BlockSpec((tm, d), row), _const_spec((1, d)), _const_spec(w["w_in"].shape),
            _const_spec((1, A_Q_LORA)), _const_spec((1, A_KV_LORA)), _const_spec(w["w_q"].shape),
            _const_spec(w["w_kv"].shape)] + [pl.BlockSpec((tm, HEAD_SLOT), pos)] * len(tables) + [_const_spec(lb_raw.shape)],
        out_specs=[kpet_spec if o is None else pl.BlockSpec((tm, o[0]), row) for o in outs],
        out_shape=[kpet_shape if o is None else jax.ShapeDtypeStruct((n, o[0]), o[1]) for o in outs],
        compiler_params=_params(("parallel",)),
        name="ab_input_stage",
    )(x, g.reshape(1, d), w["w_in"], w["g_q"], w["g_kv"], w["w_q"], w["w_kv"], *tables, lb_raw)


def _attn_prompt_body(q_ref, k_ref, v_ref, o_ref, vaug, *, tq, n_pairs):
    s_len = q_ref.shape[1]
    nq = s_len // tq
    nv = 2 * A_V
    lane = lax.broadcasted_iota(jnp.int32, (tq, nv), 1)
    r_i = lax.broadcasted_iota(jnp.int32, (tq, tq), 0)
    c_i = lax.broadcasted_iota(jnp.int32, (tq, tq), 1)
    causal = c_i <= r_i
    for pr in range(n_pairs):
        vaug[pr, :, :nv] = v_ref[0, :, pr * nv:(pr + 1) * nv]
        vaug[pr, :, nv:] = jnp.ones((s_len, nv), BF16)
    units = [(pr, qi, hh) for pr in range(n_pairs) for qi in range(nq) for hh in range(2)]

    def scores(pr, qi, hh):
        rows = slice(qi * tq, (qi + 1) * tq)
        hs = slice((2 * pr + hh) * HEAD_SLOT, (2 * pr + hh + 1) * HEAD_SLOT)
        q = q_ref[0, rows, hs]
        s_d = jnp.where(causal, _dot_nt(q, k_ref[0, rows, hs]), -jnp.inf)
        s_p = _dot_nt(q, k_ref[0, :qi * tq, hs]) if qi else None
        return s_d, s_p

    def finish(pr, qi, s_d, s_p):
        rows = slice(qi * tq, (qi + 1) * tq)
        m = jnp.max(s_d, axis=-1, keepdims=True)
        if s_p is not None:
            m = jnp.maximum(m, jnp.max(s_p, axis=-1, keepdims=True))
        o = _dot(jnp.exp2(s_d - m).astype(BF16), vaug[pr, rows, :])
        if s_p is not None:
            o = o + _dot(jnp.exp2(s_p - m).astype(BF16), vaug[pr, :qi * tq, :])
        return o[:, :nv] / o[:, nv:]

    pending, outs = {}, {}
    for idx in range(len(units) + ATTN_SCORES_AHEAD):
        if idx < len(units):
            pending[idx] = scores(*units[idx])
        j = idx - ATTN_SCORES_AHEAD
        if j >= 0:
            pr, qi, hh = units[j]
            outs[hh] = finish(pr, qi, *pending.pop(j))
            if hh == 1:
                o_ref[0, qi * tq:(qi + 1) * tq, pr * nv:(pr + 1) * nv] = (
                    jnp.where(lane < A_V, outs[0], outs[1]).astype(BF16))


def _attn_prompt(q, k, v, tq=256, n_pairs=ATTN_HEAD_PAIRS):
    b, s, _ = q.shape
    tq = _row_tile(s, tq)
    hps = 2 * n_pairs
    blk = lambda i, j: (i, 0, j)
    return pl.pallas_call(
        functools.partial(_attn_prompt_body, tq=tq, n_pairs=n_pairs),
        grid=(b, A_HEADS // hps),
        in_specs=[
            pl.BlockSpec((1, s, hps * HEAD_SLOT), blk), pl.BlockSpec((1, s, hps * HEAD_SLOT), blk),
            pl.BlockSpec((1, s, hps * A_V), blk),
        ],
        out_specs=pl.BlockSpec((1, s, hps * A_V), blk),
        out_shape=jax.ShapeDtypeStruct((b, s, A_HEADS * A_V), BF16),
        scratch_shapes=[pltpu.VMEM((n_pairs, s, 4 * A_V), BF16)],
        compiler_params=_params(("parallel", "parallel")),
        name="mla_prompt_attention",
    )(q, k, v)


def _run_phases(phases):
    prologue, segments, epilogue = phases
    prologue()
    for seg in segments:
        seg()
    epilogue()


def _hgrn_prompt_body(*refs, n_sb):
    _run_phases(_hgrn_phases(*refs, n_sb=n_sb))


def _hgrn_phases(q_ref, f_ref, i_ref, gt_ref, go_ref, o_ref, st_ref, st_scr, *, n_sb):
    t = pl.program_id(1)

    def prologue():
        @pl.when(t == 0)
        def _():
            st_scr[...] = jnp.zeros_like(st_scr)

    def epilogue():
        @pl.when(t == pl.num_programs(1) - 1)
        def _():
            for hd in range(B_HEADS):
                st_ref[0, hd] = st_scr[hd].T

    prepared = {}
    preps = [functools.partial(_hgrn_prep, q_ref, f_ref, prepared, sb=sb) for sb in range(n_sb)]
    applies = [functools.partial(_hgrn_apply, i_ref, gt_ref, go_ref, o_ref, st_scr, prepared, sb=sb)
               for sb in range(n_sb)]
    return prologue, preps + applies, epilogue


def _hgrn_level_masks():
    R = HGRN_BLOCK
    r_i = lax.broadcasted_iota(jnp.int32, (R, R), 0)
    c_i = lax.broadcasted_iota(jnp.int32, (R, R), 1)
    levels = []
    m = B_CHUNK
    while 2 * m <= R:
        sh = (2 * m).bit_length() - 1
        levels.append((m, ((r_i >> sh) == (c_i >> sh)) & ((r_i & (2 * m - 1)) >= m) & ((c_i & (2 * m - 1)) < m)))
        m *= 2
    return r_i, c_i, levels


def _hgrn_prep(q_ref, f_ref, prepared, *, sb):
    R = HGRN_BLOCK
    hb = B_HEADS * B_DK
    r_i, c_i, levels = _hgrn_level_masks()
    tri = jnp.where(c_i <= r_i, 1.0, 0.0).astype(BF16)

    rows = slice(sb * R, (sb + 1) * R)
    q = q_ref[0, rows, :].astype(F32)
    f = f_ref[0, rows, :]
    k = 1.0 - f
    lf = jnp.log(f)
    hi = lf.astype(BF16)
    r1 = lf - hi.astype(F32)
    mid = r1.astype(BF16)
    lo = (r1 - mid.astype(F32)).astype(BF16)
    G = _dot(tri, hi) + _dot(tri, mid) + _dot(tri, lo)
    g_last = G[R - 1:R, :]

    n_ch = R // B_CHUNK
    end_rows = [G[B_CHUNK * (j + 1) - 1:B_CHUNK * (j + 1), :] for j in range(n_ch)]
    g_end = jnp.concatenate(end_rows, axis=0)
    g_start = jnp.concatenate([jnp.zeros((1, hb), F32)] + end_rows[:-1], axis=0)

    def expand(per_chunk):
        return jnp.concatenate([jnp.broadcast_to(per_chunk[j:j + 1, :], (B_CHUNK, hb)) for j in range(n_ch)], axis=0)

    g0 = expand(g_start)
    q0 = q * jnp.exp(G - g0)
    k_end = k * jnp.exp(expand(g_end) - G)
    q_lv = [q0.astype(BF16)]
    k_lv = [(k * jnp.exp(g0 - G)).astype(BF16)]
    q_int = (q0 * expand(jnp.exp(g_start))).astype(BF16)
    k_st = (k_end * expand(jnp.exp(g_last - g_end))).astype(BF16)
    for mm, _ in levels:
        cpb = 2 * mm // B_CHUNK
        g_mid = jnp.concatenate([end_rows[(j // cpb) * cpb + cpb // 2 - 1] for j in range(n_ch)], axis=0)
        q_lv.append((q0 * expand(jnp.exp(jnp.minimum(g_start - g_mid, 0.0)))).astype(BF16))
        k_lv.append((k_end * expand(jnp.exp(jnp.minimum(g_mid - g_end, 0.0)))).astype(BF16))
    prepared[sb] = (q_lv, k_lv, q_int, k_st, jnp.exp(g_last))


def _hgrn_apply(i_ref, gt_ref, go_ref, o_ref, st_scr, prepared, *, sb):
    R = HGRN_BLOCK
    r_i, c_i, levels = _hgrn_level_masks()
    base_mask = ((r_i >> 4) == (c_i >> 4)) & (c_i <= r_i)
    go = go_ref[...]
    rows = slice(sb * R, (sb + 1) * R)
    v = i_ref[0, rows, :]
    q_lv, k_lv, q_int, k_st, decay = prepared[sb]

    heads = [slice(hd * B_DK, (hd + 1) * B_DK) for hd in range(B_HEADS)]
    inter, state_in, atts = [], [], []
    for hd, hs in enumerate(heads):
        st = st_scr[hd]
        inter.append(_dot_nt(q_int[:, hs], st.astype(BF16)))
        state_in.append(st * decay[:, hs] + _dot_tn(v[:, hs], k_st[:, hs]))
    for hs in heads:
        att = jnp.where(base_mask, _dot_nt(q_lv[0][:, hs], k_lv[0][:, hs]), 0.0)
        for li, (_, msk) in enumerate(levels):
            att = jnp.where(msk, _dot_nt(q_lv[li + 1][:, hs], k_lv[li + 1][:, hs]), att)
        atts.append(att.astype(BF16))
    for hd, hs in enumerate(heads):
        st_scr[hd] = state_in[hd]
        o = _dot(atts[hd], v[:, hs]) + inter[hd]
        o = o * lax.rsqrt(jnp.mean(o * o, axis=-1, keepdims=True) + RMS_EPS) * go
        o_ref[0, rows, hs] = (o * gt_ref[0, rows, hs].astype(F32)).astype(BF16)


def _hgrn_prompt(hq, hf, hi, hg, g_out, rows_per_step):
    b, s, hb = hq.shape
    assert s % HGRN_BLOCK == 0
    ts = rows_per_step
    blk = lambda i, t: (i, t, 0)
    return pl.pallas_call(
        functools.partial(_hgrn_prompt_body, n_sb=ts // HGRN_BLOCK),
        grid=(b, s // ts),
        in_specs=[pl.BlockSpec((1, ts, hb), blk)] * 4 + [pl.BlockSpec((1, B_DV), lambda i, t: (0, 0))],
        out_specs=[
            pl.BlockSpec((1, ts, hb), blk),
            pl.BlockSpec((1, B_HEADS, B_DK, B_DV), lambda i, t: (i, 0, 0, 0)),
        ],
        out_shape=[
            jax.ShapeDtypeStruct((b, s, hb), BF16),
            jax.ShapeDtypeStruct((b, B_HEADS, B_DK, B_DV), F32),
        ],
        scratch_shapes=[pltpu.VMEM((B_HEADS, B_DV, B_DK), F32)],
        compiler_params=_params(("parallel", "arbitrary")),
        name="hgrn2_prompt",
    )(hq, hf, hi, hg, g_out.reshape(1, B_DV))


def _hgrn_step_body(q_ref, f_ref, i_ref, gt_ref, go_ref, s_ref, o_ref, so_ref, *, nb):
    go = go_ref[...]
    pad = jnp.zeros((B_DK - 3 * nb, B_DK), F32)
    for hd in range(B_HEADS):
        hs = slice(hd * B_DK, (hd + 1) * B_DK)
        f = f_ref[:, hs]
        cols = jnp.concatenate([q_ref[:, hs].astype(F32), f, 1.0 - f, pad], axis=0).T
        for j in range(nb):
            s_new = cols[:, nb + j:nb + j + 1] * s_ref[j, hd] + cols[:, 2 * nb + j:2 * nb + j + 1] * i_ref[j:j + 1, hs].astype(F32)
            so_ref[j, hd] = s_new
            o = jnp.sum(cols[:, j:j + 1] * s_new, axis=0, keepdims=True)
            o = o * lax.rsqrt(jnp.mean(o * o, axis=-1, keepdims=True) + RMS_EPS) * go
            o_ref[j:j + 1, hs] = (o * gt_ref[j:j + 1, hs].astype(F32)).astype(BF16)


def _hgrn_step(hq, hf, hi, hg, g_out, state, layer, nb=8):
    n, hb = hq.shape
    nb = _row_tile(n, nb)
    row = lambda i: (i, 0)
    st_in = lambda i: (layer, i, 0, 0, 0)
    st_out = lambda i: (i, 0, 0, 0)
    return pl.pallas_call(
        functools.partial(_hgrn_step_body, nb=nb),
        grid=(n // nb,),
        in_specs=[pl.BlockSpec((nb, hb), row)] * 4 + [_const_spec((1, B_DV)), pl.BlockSpec((None, nb, B_HEADS, B_DK, B_DV), st_in)],
        out_specs=[pl.BlockSpec((nb, hb), row), pl.BlockSpec((nb, B_HEADS, B_DK, B_DV), st_out)],
        out_shape=[jax.ShapeDtypeStruct((n, hb), BF16), jax.ShapeDtypeStruct(state.shape[1:], F32)],
        compiler_params=_params(("parallel",)),
        name="hgrn2_step",
    )(hq, hf, hi, hg, g_out.reshape(1, B_DV), state)


def _small_matmul_body(a_ref, w_ref, o_ref):
    o_ref[...] = _dot(a_ref[...].astype(BF16), w_ref[...]).astype(o_ref.dtype)


def _small_matmul(a, w, out_dtype):
    n = a.shape[0]
    out_spec = pl.BlockSpec((n, w.shape[1]), lambda i: (0, 0))
    return pl.pallas_call(
        _small_matmul_body,
        grid=(1,),
        in_specs=[_const_spec(a.shape), _const_spec(w.shape)],
        out_specs=out_spec,
        out_shape=jax.ShapeDtypeStruct((n, w.shape[1]), out_dtype),
        compiler_params=_params(("arbitrary",)),
        name="sample_matmul",
    )(a, w)


def _page_copies(pt_ref, ckv_hbm, kpe_hbm, ckv_buf, kpe_buf, sems, step, slot, *, n_chunk, pc, page):
    b = step // n_chunk
    c = step % n_chunk
    copies = []
    for p in range(pc):
        pg = pt_ref[b, c * pc + p]
        copies.append(pltpu.make_async_copy(ckv_hbm.at[pg], ckv_buf.at[slot, pl.ds(p * page, page), :], sems.at[slot, 0]))
        copies.append(pltpu.make_async_copy(kpe_hbm.at[pg], kpe_buf.at[slot, :, pl.ds(p * page, page)], sems.at[slot, 1]))
    return copies


def _attn_decode_body(pt_ref, *refs, **kw):
    _run_phases(_decode_phases(pl.program_id(0), pl.num_programs(0), pt_ref, *refs, **kw))


def _decode_phases(s, total, pt_ref, ql_ref, qp_ref, cn_ref, kn_ref, ckv_hbm, kpe_hbm, o_ref,
                   ckv_buf, kpe_buf, sems, m_scr, l_scr, acc_scr, *, layer, n_chunk, pc, page):
    slot = s % 2
    c = s % n_chunk
    copies = functools.partial(_page_copies, pt_ref, ckv_hbm.at[layer], kpe_hbm.at[layer], ckv_buf, kpe_buf, sems,
                               n_chunk=n_chunk, pc=pc, page=page)

    def prologue():
        @pl.when(s == 0)
        def _():
            for cp in copies(s, slot):
                cp.start()

        @pl.when(s + 1 < total)
        def _():
            for cp in copies(s + 1, 1 - slot):
                cp.start()

        for cp in copies(s, slot):
            cp.wait()

        @pl.when(c == 0)
        def _():
            cn = cn_ref[0]
            s_self = (jnp.sum(ql_ref[0] * cn, axis=-1, keepdims=True)
                      + jnp.sum(qp_ref[0] * kn_ref[0], axis=-1, keepdims=True))
            m_scr[...] = s_self
            l_scr[...] = jnp.ones_like(l_scr)
            acc_scr[...] = jnp.broadcast_to(cn, acc_scr.shape)

    n_keys = pc * page
    part = n_keys // DECODE_KEY_PARTS if n_keys % (DECODE_KEY_PARTS * LANES) == 0 else n_keys
    n_parts = n_keys // part
    scores, stats = {}, {}

    def part_scores(i):
        ks = slice(i * part, (i + 1) * part)
        scores[i] = _dot_nt(ql_ref[0], ckv_buf[slot, ks, :]) + _dot(qp_ref[0], kpe_buf[slot, :, ks])

    def part_pv(i):
        ks = slice(i * part, (i + 1) * part)
        m_i = jnp.max(scores[i], axis=-1, keepdims=True)
        p = jnp.exp2(scores[i] - m_i)
        stats[i] = (m_i, jnp.sum(p, axis=-1, keepdims=True), _dot(p, ckv_buf[slot, ks, :]))

    def merge():
        m_old = m_scr[...]
        m_new = m_old
        for i in range(n_parts):
            m_new = jnp.maximum(m_new, stats[i][0])
        alpha = jnp.exp2(m_old - m_new)
        l_new = alpha * l_scr[...]
        acc_new = alpha * acc_scr[...]
        for i in range(n_parts):
            m_i, l_i, acc_i = stats[i]
            w_i = jnp.exp2(m_i - m_new)
            l_new = l_new + w_i * l_i
            acc_new = acc_new + w_i * acc_i
        l_scr[...] = l_new
        acc_scr[...] = acc_new
        m_scr[...] = m_new

    def epilogue():
        @pl.when(c == n_chunk - 1)
        def _():
            o_ref[0] = acc_scr[...] / l_scr[...]

    segments = ([functools.partial(part_scores, i) for i in range(n_parts)]
                + [functools.partial(part_pv, i) for i in range(n_parts)] + [merge])
    return prologue, segments, epilogue


def _decode_pages_per_step(n_pages, page):
    budget = DECODE_BUFFER_BYTES // (2 * page * (A_KV_LORA + A_ROPE) * 4)
    return max(p for p in range(1, n_pages + 1) if n_pages % p == 0 and p <= max(budget, 1))


def _attn_decode(page_table, q_lat, q_pe, ckv_new, kpe_new, pool_ckv, pool_kpe_t, layer):
    db, n_pages = page_table.shape
    page = pool_ckv.shape[2]
    pc = _decode_pages_per_step(n_pages, page)
    n_chunk = n_pages // pc
    per_row = lambda s, pt: (s // n_chunk, 0, 0)
    grid_spec = pltpu.PrefetchScalarGridSpec(
        num_scalar_prefetch=1,
        grid=(db * n_chunk,),
        in_specs=[
            pl.BlockSpec((1, A_HEADS, A_KV_LORA), per_row), pl.BlockSpec((1, A_HEADS, A_ROPE), per_row),
            pl.BlockSpec((1, 1, A_KV_LORA), per_row), pl.BlockSpec((1, 1, A_ROPE), per_row),
            pl.BlockSpec(memory_space=pl.ANY), pl.BlockSpec(memory_space=pl.ANY),
        ],
        out_specs=pl.BlockSpec((1, A_HEADS, A_KV_LORA), per_row),
        scratch_shapes=_decode_scratch(pc, page),
    )
    return pl.pallas_call(
        functools.partial(_attn_decode_body, layer=layer, n_chunk=n_chunk, pc=pc, page=page),
        grid_spec=grid_spec,
        out_shape=jax.ShapeDtypeStruct((db, A_HEADS, A_KV_LORA), F32),
        compiler_params=_params(("arbitrary",)),
        name="mla_paged_decode",
    )(page_table, q_lat, q_pe, ckv_new, kpe_new, pool_ckv, pool_kpe_t)


def _decode_scratch(pc, page):
    return [
        pltpu.VMEM((2, pc * page, A_KV_LORA), F32), pltpu.VMEM((2, A_ROPE, pc * page), F32),
        pltpu.SemaphoreType.DMA((2, 2)),
        pltpu.VMEM((A_HEADS, 1), F32), pltpu.VMEM((A_HEADS, 1), F32), pltpu.VMEM((A_HEADS, A_KV_LORA), F32),
    ]


def _hgrn_decode_body(pt_ref, q_ref, f_ref, i_ref, gt_ref, go_ref, ql_ref, qp_ref, cn_ref, kn_ref, ckv_hbm, kpe_hbm,
                      o_ref, st_ref, ol_ref, st_scr, ckv_buf, kpe_buf, sems, m_scr, l_scr, acc_scr, *, n_sb, decode_kw):
    step = pl.program_id(0) * pl.num_programs(1) + pl.program_id(1)
    total = pl.num_programs(0) * pl.num_programs(1)
    dec_pro, dec_segs, dec_epi = _decode_phases(step, total, pt_ref, ql_ref, qp_ref, cn_ref, kn_ref, ckv_hbm, kpe_hbm,
                                                 ol_ref, ckv_buf, kpe_buf, sems, m_scr, l_scr, acc_scr, **decode_kw)
    hgr_pro, hgr_segs, hgr_epi = _hgrn_phases(q_ref, f_ref, i_ref, gt_ref, go_ref, o_ref, st_ref, st_scr, n_sb=n_sb)
    dec_pro()
    hgr_pro()
    n_prep = len(hgr_segs) // 2
    n_score = (len(dec_segs) - 1) // 2
    order = (dec_segs[:1] + hgr_segs[:n_prep] + dec_segs[1:n_score]
             + _interleave(dec_segs[n_score:], hgr_segs[n_prep:]))
    for seg in order:
        seg()
    dec_epi()
    hgr_epi()


def _interleave(a, b):
    out, j = [], 0
    for i, seg in enumerate(a):
        out.append(seg)
        upto = (i + 1) * len(b) // len(a)
        out.extend(b[j:upto])
        j = upto
    return out


def _merged_chunks(n_steps, db, n_pages, page):
    if n_steps % db:
        return None
    n_chunk = n_steps // db
    if n_pages % n_chunk or n_pages // n_chunk > _decode_pages_per_step(n_pages, page):
        return None
    return n_chunk


def _hgrn_prompt_with_decode(hq, hf, hi, hg, g_out, page_table, q_lat, q_pe, ckv_new, kpe_new, pool_ckv, pool_kpe_t,
                             layer, n_chunk, rows_per_step):
    b, s, hb = hq.shape
    ts = rows_per_step
    nt = s // ts
    db, n_pages = page_table.shape
    page = pool_ckv.shape[2]
    pc = n_pages // n_chunk
    blk = lambda i, t, pt: (i, t, 0)
    per_row = lambda i, t, pt: ((i * nt + t) // n_chunk, 0, 0)
    grid_spec = pltpu.PrefetchScalarGridSpec(
        num_scalar_prefetch=1,
        grid=(b, nt),
        in_specs=[pl.BlockSpec((1, ts, hb), blk)] * 4 + [
            pl.BlockSpec((1, B_DV), lambda i, t, pt: (0, 0)),
            pl.BlockSpec((1, A_HEADS, A_KV_LORA), per_row), pl.BlockSpec((1, A_HEADS, A_ROPE), per_row),
            pl.BlockSpec((1, 1, A_KV_LORA), per_row), pl.BlockSpec((1, 1, A_ROPE), per_row),
            pl.BlockSpec(memory_space=pl.ANY), pl.BlockSpec(memory_space=pl.ANY),
        ],
        out_specs=[
            pl.BlockSpec((1, ts, hb), blk),
            pl.BlockSpec((1, B_HEADS, B_DK, B_DV), lambda i, t, pt: (i, 0, 0, 0)),
            pl.BlockSpec((1, A_HEADS, A_KV_LORA), per_row),
        ],
        scratch_shapes=[pltpu.VMEM((B_HEADS, B_DV, B_DK), F32)] + _decode_scratch(pc, page),
    )
    decode_kw = dict(layer=layer, n_chunk=n_chunk, pc=pc, page=page)
    return pl.pallas_call(
        functools.partial(_hgrn_decode_body, n_sb=ts // HGRN_BLOCK, decode_kw=decode_kw),
        grid_spec=grid_spec,
        out_shape=[
            jax.ShapeDtypeStruct((b, s, hb), BF16),
            jax.ShapeDtypeStruct((b, B_HEADS, B_DK, B_DV), F32),
            jax.ShapeDtypeStruct((db, A_HEADS, A_KV_LORA), F32),
        ],
        compiler_params=_params(("arbitrary", "arbitrary")),
        name="hgrn2_prompt_with_decode",
    )(page_table, hq, hf, hi, hg, g_out.reshape(1, B_DV), q_lat, q_pe, ckv_new, kpe_new, pool_ckv, pool_kpe_t)


def _cmix_prompt_body(x_ref, g_ref, win_ref, wc_ref, wout_ref, o_ref, buf_ref, carry, *, n_tiles):
    t = pl.program_id(1)
    tm = x_ref.shape[1]
    cw = wout_ref.shape[0]

    @pl.when(t == 0)
    def _():
        carry[...] = jnp.zeros_like(carry)

    x = x_ref[0]
    h = _rms(x, g_ref[...]).astype(BF16)
    p = _dot(h, win_ref[...])
    v = p[:, cw:2 * cw] * p[:, 2 * cw:]
    row = lax.broadcasted_iota(jnp.int32, v.shape, 0)
    prev1 = jnp.broadcast_to(carry[1:2, :], v.shape)
    prev2 = jnp.broadcast_to(carry[0:1, :], v.shape)
    v1 = jnp.where(row == 0, prev1, pltpu.roll(v, 1, 0))
    v2 = jnp.where(row == 0, prev2, jnp.where(row == 1, prev1, pltpu.roll(v, 2, 0)))
    y = wc_ref[0:1, :] * v2 + wc_ref[1:2, :] * v1 + wc_ref[2:3, :] * v
    o_ref[0] = x + _dot((p[:, :cw] * y).astype(BF16), wout_ref[...])
    carry[0:2, :] = v[tm - 2:, :]

    @pl.when(t == n_tiles - 1)
    def _():
        buf_ref[0] = v[tm - 2:, :]


def _cmix_prompt(x, g, w_in, w_conv, w_out, tm=1024):
    b, s, d = x.shape
    tm = _row_tile(s, tm)
    n_tiles = s // tm
    cw = w_out.shape[0]
    blk = lambda i, t: (i, t, 0)
    cst = lambda shape: pl.BlockSpec(shape, lambda i, t: (0,) * len(shape), pipeline_mode=pl.Buffered(1))
    return pl.pallas_call(
        functools.partial(_cmix_prompt_body, n_tiles=n_tiles),
        grid=(b, n_tiles),
        in_specs=[pl.BlockSpec((1, tm, d), blk), cst((1, d)), cst(w_in.shape), cst(w_conv.shape), cst(w_out.shape)],
        out_specs=[pl.BlockSpec((1, tm, d), blk), pl.BlockSpec((1, C_KERNEL - 1, cw), lambda i, t: (i, 0, 0))],
        out_shape=[jax.ShapeDtypeStruct((b, s, d), F32), jax.ShapeDtypeStruct((b, C_KERNEL - 1, cw), F32)],
        scratch_shapes=[pltpu.VMEM((8, cw), F32)],
        compiler_params=_params(("parallel", "arbitrary")),
        name="conv_mixer_prompt",
    )(x, g.reshape(1, d), w_in, w_conv, w_out)


def _cmix_step_body(x_ref, b0_ref, b1_ref, g_ref, win_ref, wc_ref, wout_ref, o_ref, v_ref):
    cw = wout_ref.shape[0]
    x = x_ref[...]
    h = _rms(x, g_ref[...]).astype(BF16)
    p = _dot(h, win_ref[...])
    v = p[:, cw:2 * cw] * p[:, 2 * cw:]
    y = wc_ref[0:1, :] * b0_ref[...] + wc_ref[1:2, :] * b1_ref[...] + wc_ref[2:3, :] * v
    o_ref[...] = x + _dot((p[:, :cw] * y).astype(BF16), wout_ref[...])
    v_ref[...] = v


def _cmix_step(x, buf0, buf1, g, w_in, w_conv, w_out):
    n, d = x.shape
    cw = w_out.shape[0]
    args = (x, buf0, buf1, g.reshape(1, d), w_in, w_conv, w_out)
    return pl.pallas_call(
        _cmix_step_body,
        grid=(1,),
        in_specs=[_const_spec(a.shape) for a in args],
        out_specs=[pl.BlockSpec((n, d), lambda i: (0, 0)), pl.BlockSpec((n, cw), lambda i: (0, 0))],
        out_shape=[jax.ShapeDtypeStruct((n, d), F32), jax.ShapeDtypeStruct((n, cw), F32)],
        compiler_params=_params(("arbitrary",)),
        name="conv_mixer_step",
    )(*args)


def _prep_ab_weights(w_in, g_q, g_kv, w_uq, w_ukv, w_out):
    d = w_in.shape[0]
    o_kv = A_Q_LORA
    o_pe = o_kv + A_KV_LORA
    o_h = o_pe + A_ROPE
    zn = jnp.zeros((d, A_NOPE), F32)
    zt = jnp.zeros((d, HEAD_SLOT - A_NOPE - A_ROPE), F32)
    w_in_p = jnp.concatenate([w_in[:, :o_pe], zn, w_in[:, o_pe:o_h], zt, w_in[:, o_h:]], axis=1).astype(BF16)

    wq = w_uq.reshape(A_Q_LORA, A_HEADS, A_NOPE + A_ROPE)
    zq = jnp.zeros((A_Q_LORA, A_HEADS, HEAD_SLOT - A_NOPE - A_ROPE), F32)
    w_q = jnp.concatenate([wq, zq], axis=-1).reshape(A_Q_LORA, A_HEADS * HEAD_SLOT).astype(BF16)

    wkv = w_ukv.reshape(A_KV_LORA, A_HEADS, A_NOPE + A_V)
    w_uk, w_uv = wkv[..., :A_NOPE], wkv[..., A_NOPE:]
    k_pad = jnp.concatenate([w_uk, jnp.zeros((A_KV_LORA, A_HEADS, HEAD_SLOT - A_NOPE), F32)], axis=-1)
    w_kv = jnp.concatenate(
        [k_pad.reshape(A_KV_LORA, A_HEADS * HEAD_SLOT), w_uv.reshape(A_KV_LORA, A_HEADS * A_V)], axis=1).astype(BF16)

    eye = jnp.eye(A_HEADS, dtype=F32)
    uk_t = jnp.transpose(w_uk, (1, 2, 0))
    uk_t = jnp.concatenate([uk_t, jnp.zeros((A_HEADS, HEAD_SLOT - A_NOPE, A_KV_LORA), F32)], axis=1)
    absorb = (uk_t[:, :, None, :] * eye[:, None, :, None]).reshape(A_HEADS * HEAD_SLOT, A_HEADS * A_KV_LORA)
    sel = jnp.zeros((HEAD_SLOT, A_ROPE), F32).at[A_NOPE + jnp.arange(A_ROPE), jnp.arange(A_ROPE)].set(1.0)
    sel = (sel[None, :, None, :] * eye[:, None, :, None]).reshape(A_HEADS * HEAD_SLOT, A_HEADS * A_ROPE)
    w_dec_q = jnp.concatenate([absorb, sel], axis=1).astype(BF16)
    uv_t = jnp.transpose(w_uv, (1, 0, 2))
    w_dec_o = (uv_t[:, :, None, :] * eye[:, None, :, None]).reshape(A_HEADS * A_KV_LORA, A_HEADS * A_V).astype(BF16)

    n_a = A_HEADS * A_V
    return dict(w_in=w_in_p, g_q=g_q.reshape(1, -1), g_kv=g_kv.reshape(1, -1), w_q=w_q, w_kv=w_kv,
                w_dec_q=w_dec_q, w_dec_o=w_dec_o, w_out_a=w_out[:n_a].astype(BF16), w_out_b=w_out[n_a:].astype(BF16))


def _rope_tables(pos):
    inv = ROPE_BASE ** (-jnp.arange(0, A_ROPE, 2, dtype=F32) / A_ROPE)
    ang = pos.astype(F32)[:, None] * inv[None, :]
    cos, sin = jnp.cos(ang), jnp.sin(ang)
    n = pos.shape[0]
    zero = jnp.zeros_like(sin)
    head = jnp.zeros((n, A_NOPE), F32)
    tail = jnp.zeros((n, HEAD_SLOT - A_NOPE - A_ROPE), F32)
    cos_t = jnp.concatenate([jnp.ones((n, A_NOPE), F32), cos, cos, tail], axis=1)
    sin_lo = jnp.concatenate([head, -sin, zero, tail], axis=1)
    sin_hi = jnp.concatenate([head, zero, sin, tail], axis=1)
    c = SM_SCALE * LOG2_E
    return cos_t * c, sin_lo * c, sin_hi * c, cos_t, sin_lo, sin_hi


def kernel(x_prompt, x_sample, cache_ckv, cache_kpe, state_hgrn, state_conv, page_table, norm_g, final_norm_g,
           w_ffn_gate, w_ffn_up, w_ffn_down, w_in_ab, g_q_lora, g_kv_lora, w_uq, w_ukv, hgrn_lower_bound,
           g_hgrn_out, w_out_ab, w_in_c, w_conv_c, w_out_c):
    b_p, s_p, d = x_prompt.shape
    d_b, s_d, _ = x_sample.shape
    assert s_d == 1
    depth = norm_g.shape[0]
    past_len = page_table.shape[1] * cache_ckv.shape[2]
    n_p = b_p * s_p

    tab_p = _rope_tables(jnp.arange(s_p))
    tab_d = _rope_tables(jnp.full((d_b,), past_len))
    pool_kpe_t = jnp.swapaxes(cache_kpe, 2, 3)

    w_ffn = (w_ffn_gate.astype(BF16), w_ffn_up.astype(BF16), w_ffn_down.astype(BF16))

    xp = x_prompt.reshape(n_p, d)
    xd = x_sample.reshape(d_b, d)
    outs = {k: [] for k in ("ckv_p", "kpe_p", "ckv_d", "kpe_d", "hg_p", "hg_d", "cv_p", "cv_d")}

    for layer in range(depth):
        j = layer // 2
        last = layer == depth - 1
        xp = _ffn(xp, norm_g[layer, 0], w_ffn, (layer, 0))
        xd = _ffn(xd, norm_g[layer, 0], w_ffn, (layer, 0))
        ffn_b = (norm_g[layer, 2], w_ffn, (layer, 1))
        fin = final_norm_g if last else None
        if layer % 2 == 0:
            w = _prep_ab_weights(w_in_ab[j], g_q_lora[j], g_kv_lora[j], w_uq[j], w_ukv[j], w_out_ab[j])
            ckv, kpe_t, q, k, v, hq, hf, hi, hg = _ab_pre(xp, norm_g[layer, 1], w, tab_p, hgrn_lower_bound, layer)
            outs["ckv_p"].append(ckv.reshape(b_p, s_p, -1))
            outs["kpe_p"].append(jnp.swapaxes(kpe_t, 1, 2))
            o_a = _attn_prompt(q.reshape(b_p, s_p, -1), k.reshape(b_p, s_p, -1), v.reshape(b_p, s_p, -1))
            sh = (b_p, s_p, -1)
            hgrn_in = (hq.reshape(sh), hf.reshape(sh), hi.reshape(sh), hg.reshape(sh), g_hgrn_out[j])
            ckv, kpe_t, q, _, _, hq, hf, hi, hg = _ab_pre(xd, norm_g[layer, 1], w, tab_d, hgrn_lower_bound, layer)
            kpe = kpe_t[0].T
            qd = _small_matmul(q, w["w_dec_q"], F32)
            n_lat = A_HEADS * A_KV_LORA
            dec_in = (page_table, qd[:, :n_lat].reshape(d_b, A_HEADS, A_KV_LORA), qd[:, n_lat:].reshape(d_b, A_HEADS, A_ROPE),
                      ckv.reshape(d_b, 1, -1), kpe.reshape(d_b, 1, -1), cache_ckv, pool_kpe_t, j)
            ts = _row_tile(s_p, HGRN_ROWS_PER_STEP)
            n_chunk = _merged_chunks(b_p * (s_p // ts), d_b, page_table.shape[1], cache_ckv.shape[2])
            if n_chunk is None:
                o_b_p, st_p = _hgrn_prompt(*hgrn_in, rows_per_step=ts)
                o_lat = _attn_decode(*dec_in)
            else:
                o_b_p, st_p, o_lat = _hgrn_prompt_with_decode(*hgrn_in, *dec_in, n_chunk, ts)
            outs["hg_p"].append(st_p)
            pre = ((o_a.reshape(n_p, -1), w["w_out_a"]), (o_b_p.reshape(n_p, -1), w["w_out_b"]))
            xp = _ffn(xp, *ffn_b, pre=pre, final_g=fin)
            o_a = _small_matmul(o_lat.reshape(d_b, n_lat), w["w_dec_o"], BF16)
            o_b, st_d = _hgrn_step(hq, hf, hi, hg, g_hgrn_out[j], state_hgrn, j)
            xd = _ffn(xd, *ffn_b, pre=((o_a, w["w_out_a"]), (o_b, w["w_out_b"])), final_g=fin)
            outs["ckv_d"].append(ckv)
            outs["kpe_d"].append(kpe)
            outs["hg_d"].append(st_d)
        else:
            win = w_in_c[j].astype(BF16)
            wout = w_out_c[j].astype(BF16)
            xp3, buf_p = _cmix_prompt(xp.reshape(b_p, s_p, d), norm_g[layer, 1], win, w_conv_c[j], wout)
            xp = _ffn(xp3.reshape(n_p, d), *ffn_b, final_g=fin)
            xd, v_new = _cmix_step(xd, state_conv[j, :, 0], state_conv[j, :, 1], norm_g[layer, 1], win, w_conv_c[j], wout)
            xd = _ffn(xd, *ffn_b, final_g=fin)
            outs["cv_p"].append(buf_p)
            outs["cv_d"].append(jnp.stack([state_conv[j, :, 1], v_new], axis=1))

    return (xp.reshape(b_p, s_p, d), xd.reshape(d_b, s_d, d),
            jnp.stack(outs["ckv_p"]), jnp.stack(outs["kpe_p"]),
            jnp.stack(outs["ckv_d"]).reshape(-1, d_b, s_d, A_KV_LORA), jnp.stack(outs["kpe_d"]).reshape(-1, d_b, s_d, A_ROPE),
            jnp.stack(outs["hg_p"]), jnp.stack(outs["hg_d"]), jnp.stack(outs["cv_p"]), jnp.stack(outs["cv_d"]))
```

```python
import functools

import jax
import jax.numpy as jnp
from jax import lax
from jax.experimental import pallas as pl
from jax.experimental.pallas import tpu as pltpu

F32 = jnp.float32
BF16 = jnp.bfloat16

A_HEADS = 8
A_NOPE = 64
A_ROPE = 32
A_V = 64
A_Q_LORA = 384
A_KV_LORA = 256
ROPE_BASE = 10000.0
SM_SCALE = (A_NOPE + A_ROPE) ** -0.5
LOG2_E = 1.4426950408889634
B_HEADS = 4
B_DK = 128
B_DV = 128
B_CHUNK = 16
C_KERNEL = 3
RMS_EPS = 1e-6

LANES = 128
HEAD_SLOT = LANES
HGRN_BLOCK = 128
HGRN_ROWS_PER_STEP = 512
ATTN_HEAD_PAIRS = 4
ATTN_SCORES_AHEAD = 1
VMEM_LIMIT = 56 * 1024 * 1024
DECODE_BUFFER_BYTES = 40 * 1024 * 1024
DECODE_KEY_PARTS = 2


def _params(sem, vmem=VMEM_LIMIT):
    return pltpu.CompilerParams(dimension_semantics=sem, vmem_limit_bytes=vmem)


def _const_spec(shape):
    nd = len(shape)
    return pl.BlockSpec(shape, lambda *_: (0,) * nd, pipeline_mode=pl.Buffered(1))


def _rms(x, g):
    return x * lax.rsqrt(jnp.mean(x * x, axis=-1, keepdims=True) + RMS_EPS) * g


def _dot(a, b):
    return jnp.dot(a, b, preferred_element_type=F32)


def _dot_nt(a, b):
    return lax.dot_general(a, b, (((1,), (1,)), ((), ())), preferred_element_type=F32)


def _dot_tn(a, b):
    return lax.dot_general(a, b, (((0,), (0,)), ((), ())), preferred_element_type=F32)


def _silu(x):
    return x * jax.nn.sigmoid(x)


def _row_tile(n, want):
    t = min(n, want)
    assert n % t == 0, (n, t)
    return t


def _ffn_body(*refs, n_chunks, tf, n_pre, has_final):
    x_ref = refs[0]
    pre = refs[1:1 + 2 * n_pre]
    g_ref, wg_ref, wu_ref, wd_ref = refs[1 + 2 * n_pre:5 + 2 * n_pre]
    rest = refs[5 + 2 * n_pre:]
    gf_ref = rest[0] if has_final else None
    o_ref = rest[-1]

    x = x_ref[...]
    for p in range(n_pre):
        x = x + _dot(pre[2 * p][...], pre[2 * p + 1][...])
    h = _rms(x, g_ref[...]).astype(BF16)
    acc = None
    for c in range(n_chunks):
        cs = slice(c * tf, (c + 1) * tf)
        gate = _dot(h, wg_ref[:, cs])
        up = _dot(h, wu_ref[:, cs])
        a = (_silu(gate) * up).astype(BF16)
        part = _dot(a, wd_ref[cs, :])
        acc = part if acc is None else acc + part
    y = x + 0.5 * acc
    if has_final:
        y = _rms(y, gf_ref[...])
    o_ref[...] = y


def _ffn(x, g, weights, which, pre=(), final_g=None, tm=1024):
    n, d = x.shape
    f = weights[0].shape[-1]
    tf = 256 if f % 256 == 0 else LANES
    tm = _row_tile(n, tm)
    row = lambda i: (i, 0)
    in_specs = [pl.BlockSpec((tm, d), row)]
    args = [x]
    for act, w in pre:
        in_specs += [pl.BlockSpec((tm, act.shape[1]), row), _const_spec(w.shape)]
        args += [act, w]
    pick = lambda i: (*which, 0, 0)
    w_specs = [pl.BlockSpec((None, None) + w.shape[2:], pick, pipeline_mode=pl.Buffered(1)) for w in weights]
    in_specs += [_const_spec((1, d))] + w_specs
    args += [g.reshape(1, d), *weights]
    if final_g is not None:
        in_specs.append(_const_spec((1, d)))
        args.append(final_g.reshape(1, d))
    body = functools.partial(_ffn_body, n_chunks=f // tf, tf=tf, n_pre=len(pre), has_final=final_g is not None)
    return pl.pallas_call(
        body,
        grid=(n // tm,),
        in_specs=in_specs,
        out_specs=pl.BlockSpec((tm, d), row),
        out_shape=jax.ShapeDtypeStruct((n, d), F32),
        compiler_params=_params(("parallel",)),
        name="half_ffn",
    )(*args)


def _rope_slot(x, cos, sin_lo, sin_hi):
    half = A_ROPE // 2
    return x * cos + pltpu.roll(x, HEAD_SLOT - half, 1) * sin_lo + pltpu.roll(x, half, 1) * sin_hi


def _ab_pre_body(x_ref, g_ref, win_ref, gq_ref, gkv_ref, wq_ref, wkv_ref, cq_ref, sq_lo_ref, sq_hi_ref,
                 c_ref, s_lo_ref, s_hi_ref, lb_ref,
                 ckv_ref, kpet_ref, q_ref, k_ref, v_ref, hq_ref, hf_ref, hi_ref, hg_ref, *, layer):
    hb = B_HEADS * B_DK
    hw = A_HEADS * HEAD_SLOT
    o_kv = A_Q_LORA
    o_pe = o_kv + A_KV_LORA
    o_h = o_pe + HEAD_SLOT
    h = _rms(x_ref[...], g_ref[...]).astype(BF16)
    proj = _dot(h, win_ref[...])
    q_tabs = (cq_ref[...], sq_lo_ref[...], sq_hi_ref[...])

    cq = _rms(proj[:, :A_Q_LORA], gq_ref[...]).astype(BF16)
    qq = _dot(cq, wq_ref[...])
    for hd in range(A_HEADS):
        sl = slice(hd * HEAD_SLOT, (hd + 1) * HEAD_SLOT)
        q_ref[:, sl] = _rope_slot(qq[:, sl], *q_tabs).astype(BF16)

    ckv = _rms(proj[:, o_kv:o_pe], gkv_ref[...])
    ckv_ref[...] = ckv
    kpe = _rope_slot(proj[:, o_pe:o_h], c_ref[...], s_lo_ref[...], s_hi_ref[...])
    kpet_ref[0] = kpe.T[A_NOPE:A_NOPE + A_ROPE, :]
    kv = _dot(ckv.astype(BF16), wkv_ref[...])
    for hd in range(A_HEADS):
        sl = slice(hd * HEAD_SLOT, (hd + 1) * HEAD_SLOT)
        k_ref[:, sl] = (kv[:, sl] + kpe).astype(BF16)
    v_ref[...] = kv[:, hw:].astype(BF16)

    lbr = lb_ref[...]
    e = jnp.exp(lbr - jnp.max(lbr, axis=0, keepdims=True))
    lb = jnp.sum(e[:layer + 1], axis=0, keepdims=True) / jnp.sum(e, axis=0, keepdims=True)
    hq_ref[...] = _silu(proj[:, o_h:o_h + hb]) * (B_DK ** -0.5)
    hf_ref[...] = lb + (1.0 - lb) * jax.nn.sigmoid(proj[:, o_h + hb:o_h + 2 * hb])
    hi_ref[...] = proj[:, o_h + 2 * hb:o_h + 3 * hb].astype(BF16)
    hg_ref[...] = _silu(proj[:, o_h + 3 * hb:o_h + 4 * hb])


def _ab_pre(x, g, w, tables, lb_raw, layer, tm=1024):
    n, d = x.shape
    s_len = tables[0].shape[0]
    tm = _row_tile(s_len, tm)
    assert n % s_len == 0
    n_pos_tiles = s_len // tm
    row = lambda i: (i, 0)
    pos = lambda i: (i % n_pos_tiles, 0)
    hw = A_HEADS * HEAD_SLOT
    hb = B_HEADS * B_DK
    outs = [
        (A_KV_LORA, F32), None, (hw, BF16), (hw, BF16), (A_HEADS * A_V, BF16),
        (hb, F32), (hb, F32), (hb, BF16), (hb, F32),
    ]
    kpet_spec = pl.BlockSpec((1, A_ROPE, tm), lambda i: (i // n_pos_tiles, 0, i % n_pos_tiles))
    kpet_shape = jax.ShapeDtypeStruct((n // s_len, A_ROPE, s_len), F32)
    return pl.pallas_call(
        functools.partial(_ab_pre_body, layer=layer),
        grid=(n // tm,),
        in_specs=[
            pl.BlockSpec((tm, d), row), _const_spec((1, d)), _const_spec(w["w_in"].shape),
            _const_spec((1, A_Q_LORA)), _const_spec((1, A_KV_LORA)), _const_spec(w["w_q"].shape),
            _const_spec(w["w_kv"].shape)] + [pl.BlockSpec((tm, HEAD_SLOT), pos)] * len(tables) + [_const_spec(lb_raw.shape)],
        out_specs=[kpet_spec if o is None else pl.BlockSpec((tm, o[0]), row) for o in outs],
        out_shape=[kpet_shape if o is None else jax.ShapeDtypeStruct((n, o[0]), o[1]) for o in outs],
        compiler_params=_params(("parallel",)),
        name="ab_input_stage",
    )(x, g.reshape(1, d), w["w_in"], w["g_q"], w["g_kv"], w["w_q"], w["w_kv"], *tables, lb_raw)


def _attn_prompt_body(q_ref, k_ref, v_ref, o_ref, vaug, *, tq, n_pairs):
    s_len = q_ref.shape[1]
    nq = s_len // tq
    nv = 2 * A_V
    lane = lax.broadcasted_iota(jnp.int32, (tq, nv), 1)
    r_i = lax.broadcasted_iota(jnp.int32, (tq, tq), 0)
    c_i = lax.broadcasted_iota(jnp.int32, (tq, tq), 1)
    causal = c_i <= r_i
    for pr in range(n_pairs):
        vaug[pr, :, :nv] = v_ref[0, :, pr * nv:(pr + 1) * nv]
        vaug[pr, :, nv:] = jnp.ones((s_len, nv), BF16)
    units = [(pr, qi, hh) for pr in range(n_pairs) for qi in range(nq) for hh in range(2)]

    def scores(pr, qi, hh):
        rows = slice(qi * tq, (qi + 1) * tq)
        hs = slice((2 * pr + hh) * HEAD_SLOT, (2 * pr + hh + 1) * HEAD_SLOT)
        q = q_ref[0, rows, hs]
        s_d = jnp.where(causal, _dot_nt(q, k_ref[0, rows, hs]), -jnp.inf)
        s_p = _dot_nt(q, k_ref[0, :qi * tq, hs]) if qi else None
        return s_d, s_p

    def finish(pr, qi, s_d, s_p):
        rows = slice(qi * tq, (qi + 1) * tq)
        m = jnp.max(s_d, axis=-1, keepdims=True)
        if s_p is not None:
            m = jnp.maximum(m, jnp.max(s_p, axis=-1, keepdims=True))
        o = _dot(jnp.exp2(s_d - m).astype(BF16), vaug[pr, rows, :])
        if s_p is not None:
            o = o + _dot(jnp.exp2(s_p - m).astype(BF16), vaug[pr, :qi * tq, :])
        return o[:, :nv] / o[:, nv:]

    pending, outs = {}, {}
    for idx in range(len(units) + ATTN_SCORES_AHEAD):
        if idx < len(units):
            pending[idx] = scores(*units[idx])
        j = idx - ATTN_SCORES_AHEAD
        if j >= 0:
            pr, qi, hh = units[j]
            outs[hh] = finish(pr, qi, *pending.pop(j))
            if hh == 1:
                o_ref[0, qi * tq:(qi + 1) * tq, pr * nv:(pr + 1) * nv] = (
                    jnp.where(lane < A_V, outs[0], outs[1]).astype(BF16))


def _attn_prompt(q, k, v, tq=256, n_pairs=ATTN_HEAD_PAIRS):
    b, s, _ = q.shape
    tq = _row_tile(s, tq)
    hps = 2 * n_pairs
    blk = lambda i, j: (i, 0, j)
    return pl.pallas_call(
        functools.partial(_attn_prompt_body, tq=tq, n_pairs=n_pairs),
        grid=(b, A_HEADS // hps),
        in_specs=[
            pl.BlockSpec((1, s, hps * HEAD_SLOT), blk), pl.BlockSpec((1, s, hps * HEAD_SLOT), blk),
            pl.BlockSpec((1, s, hps * A_V), blk),
        ],
        out_specs=pl.BlockSpec((1, s, hps * A_V), blk),
        out_shape=jax.ShapeDtypeStruct((b, s, A_HEADS * A_V), BF16),
        scratch_shapes=[pltpu.VMEM((n_pairs, s, 4 * A_V), BF16)],
        compiler_params=_params(("parallel", "parallel")),
        name="mla_prompt_attention",
    )(q, k, v)


def _run_phases(phases):
    prologue, segments, epilogue = phases
    prologue()
    for seg in segments:
        seg()
    epilogue()


def _hgrn_prompt_body(*refs, n_sb):
    _run_phases(_hgrn_phases(*refs, n_sb=n_sb))


def _hgrn_phases(q_ref, f_ref, i_ref, gt_ref, go_ref, o_ref, st_ref, st_scr, *, n_sb):
    t = pl.program_id(1)

    def prologue():
        @pl.when(t == 0)
        def _():
            st_scr[...] = jnp.zeros_like(st_scr)

    def epilogue():
        @pl.when(t == pl.num_programs(1) - 1)
        def _():
            for hd in range(B_HEADS):
                st_ref[0, hd] = st_scr[hd].T

    prepared = {}
    preps = [functools.partial(_hgrn_prep, q_ref, f_ref, prepared, sb=sb) for sb in range(n_sb)]
    applies = [functools.partial(_hgrn_apply, i_ref, gt_ref, go_ref, o_ref, st_scr, prepared, sb=sb)
               for sb in range(n_sb)]
    return prologue, preps + applies, epilogue


def _hgrn_level_masks():
    R = HGRN_BLOCK
    r_i = lax.broadcasted_iota(jnp.int32, (R, R), 0)
    c_i = lax.broadcasted_iota(jnp.int32, (R, R), 1)
    levels = []
    m = B_CHUNK
    while 2 * m <= R:
        sh = (2 * m).bit_length() - 1
        levels.append((m, ((r_i >> sh) == (c_i >> sh)) & ((r_i & (2 * m - 1)) >= m) & ((c_i & (2 * m - 1)) < m)))
        m *= 2
    return r_i, c_i, levels


def _hgrn_prep(q_ref, f_ref, prepared, *, sb):
    R = HGRN_BLOCK
    hb = B_HEADS * B_DK
    _, _, levels = _hgrn_level_masks()

    rows = slice(sb * R, (sb + 1) * R)
    q = q_ref[0, rows, :]
    f = f_ref[0, rows, :]
    k = 1.0 - f
    G = jnp.log(f)
    row = lax.broadcasted_iota(jnp.int32, G.shape, 0)
    shift = 1
    while shift < R:
        G = G + jnp.where(row >= shift, pltpu.roll(G, shift, 0), 0.0)
        shift *= 2
    g_last = G[R - 1:R, :]

    n_ch = R // B_CHUNK
    end_rows = [G[B_CHUNK * (j + 1) - 1:B_CHUNK * (j + 1), :] for j in range(n_ch)]
    g_end = jnp.concatenate(end_rows, axis=0)
    g_start = jnp.concatenate([jnp.zeros((1, hb), F32)] + end_rows[:-1], axis=0)

    def expand(per_chunk):
        return jnp.concatenate([jnp.broadcast_to(per_chunk[j:j + 1, :], (B_CHUNK, hb)) for j in range(n_ch)], axis=0)

    g0 = expand(g_start)
    q0 = q * jnp.exp(G - g0)
    k_end = k * jnp.exp(expand(g_end) - G)
    q_lv = [q0.astype(BF16)]
    k_lv = [(k * jnp.exp(g0 - G)).astype(BF16)]
    q_int = (q0 * expand(jnp.exp(g_start))).astype(BF16)
    k_st = (k_end * expand(jnp.exp(g_last - g_end))).astype(BF16)
    for mm, _ in levels:
        cpb = 2 * mm // B_CHUNK
        g_mid = jnp.concatenate([end_rows[(j // cpb) * cpb + cpb // 2 - 1] for j in range(n_ch)], axis=0)
        q_lv.append((q0 * expand(jnp.exp(jnp.minimum(g_start - g_mid, 0.0)))).astype(BF16))
        k_lv.append((k_end * expand(jnp.exp(jnp.minimum(g_mid - g_end, 0.0)))).astype(BF16))
    prepared[sb] = (q_lv, k_lv, q_int, k_st, jnp.exp(g_last))


def _hgrn_apply(i_ref, gt_ref, go_ref, o_ref, st_scr, prepared, *, sb):
    R = HGRN_BLOCK
    r_i, c_i, levels = _hgrn_level_masks()
    base_mask = ((r_i >> 4) == (c_i >> 4)) & (c_i <= r_i)
    go = go_ref[...]
    rows = slice(sb * R, (sb + 1) * R)
    v = i_ref[0, rows, :]
    q_lv, k_lv, q_int, k_st, decay = prepared[sb]

    heads = [slice(hd * B_DK, (hd + 1) * B_DK) for hd in range(B_HEADS)]
    inter, state_in, atts = [], [], []
    for hd, hs in enumerate(heads):
        st = st_scr[hd]
        inter.append(_dot_nt(q_int[:, hs], st.astype(BF16)))
        state_in.append(st * decay[:, hs] + _dot_tn(v[:, hs], k_st[:, hs]))
    for hs in heads:
        att = jnp.where(base_mask, _dot_nt(q_lv[0][:, hs], k_lv[0][:, hs]), 0.0)
        for li, (_, msk) in enumerate(levels):
            att = jnp.where(msk, _dot_nt(q_lv[li + 1][:, hs], k_lv[li + 1][:, hs]), att)
        atts.append(att.astype(BF16))
    for hd, hs in enumerate(heads):
        st_scr[hd] = state_in[hd]
        o = _dot(atts[hd], v[:, hs]) + inter[hd]
        o = o * lax.rsqrt(jnp.mean(o * o, axis=-1, keepdims=True) + RMS_EPS) * go
        o_ref[0, rows, hs] = (o * gt_ref[0, rows, hs]).astype(BF16)


def _hgrn_prompt(hq, hf, hi, hg, g_out, rows_per_step):
    b, s, hb = hq.shape
    assert s % HGRN_BLOCK == 0
    ts = rows_per_step
    blk = lambda i, t: (i, t, 0)
    return pl.pallas_call(
        functools.partial(_hgrn_prompt_body, n_sb=ts // HGRN_BLOCK),
        grid=(b, s // ts),
        in_specs=[pl.BlockSpec((1, ts, hb), blk)] * 4 + [pl.BlockSpec((1, B_DV), lambda i, t: (0, 0))],
        out_specs=[
            pl.BlockSpec((1, ts, hb), blk),
            pl.BlockSpec((1, B_HEADS, B_DK, B_DV), lambda i, t: (i, 0, 0, 0)),
        ],
        out_shape=[
            jax.ShapeDtypeStruct((b, s, hb), BF16),
            jax.ShapeDtypeStruct((b, B_HEADS, B_DK, B_DV), F32),
        ],
        scratch_shapes=[pltpu.VMEM((B_HEADS, B_DV, B_DK), F32)],
        compiler_params=_params(("parallel", "arbitrary")),
        name="hgrn2_prompt",
    )(hq, hf, hi, hg, g_out.reshape(1, B_DV))


def _hgrn_step_body(q_ref, f_ref, i_ref, gt_ref, go_ref, s_ref, o_ref, so_ref, *, nb):
    go = go_ref[...]
    pad = jnp.zeros((B_DK - 3 * nb, B_DK), F32)
    for hd in range(B_HEADS):
        hs = slice(hd * B_DK, (hd + 1) * B_DK)
        f = f_ref[:, hs]
        cols = jnp.concatenate([q_ref[:, hs], f, 1.0 - f, pad], axis=0).T
        for j in range(nb):
            s_new = cols[:, nb + j:nb + j + 1] * s_ref[j, hd] + cols[:, 2 * nb + j:2 * nb + j + 1] * i_ref[j:j + 1, hs].astype(F32)
            so_ref[j, hd] = s_new
            o = jnp.sum(cols[:, j:j + 1] * s_new, axis=0, keepdims=True)
            o = o * lax.rsqrt(jnp.mean(o * o, axis=-1, keepdims=True) + RMS_EPS) * go
            o_ref[j:j + 1, hs] = (o * gt_ref[j:j + 1, hs]).astype(BF16)


def _hgrn_step(hq, hf, hi, hg, g_out, state, layer, nb=8):
    n, hb = hq.shape
    nb = _row_tile(n, nb)
    row = lambda i: (i, 0)
    st_in = lambda i: (layer, i, 0, 0, 0)
    st_out = lambda i: (i, 0, 0, 0)
    return pl.pallas_call(
        functools.partial(_hgrn_step_body, nb=nb),
        grid=(n // nb,),
        in_specs=[pl.BlockSpec((nb, hb), row)] * 4 + [_const_spec((1, B_DV)), pl.BlockSpec((None, nb, B_HEADS, B_DK, B_DV), st_in)],
        out_specs=[pl.BlockSpec((nb, hb), row), pl.BlockSpec((nb, B_HEADS, B_DK, B_DV), st_out)],
        out_shape=[jax.ShapeDtypeStruct((n, hb), BF16), jax.ShapeDtypeStruct(state.shape[1:], F32)],
        compiler_params=_params(("parallel",)),
        name="hgrn2_step",
    )(hq, hf, hi, hg, g_out.reshape(1, B_DV), state)


def _small_matmul_body(a_ref, w_ref, o_ref):
    o_ref[...] = _dot(a_ref[...].astype(BF16), w_ref[...]).astype(o_ref.dtype)


def _small_matmul(a, w, out_dtype):
    n = a.shape[0]
    out_spec = pl.BlockSpec((n, w.shape[1]), lambda i: (0, 0))
    return pl.pallas_call(
        _small_matmul_body,
        grid=(1,),
        in_specs=[_const_spec(a.shape), _const_spec(w.shape)],
        out_specs=out_spec,
        out_shape=jax.ShapeDtypeStruct((n, w.shape[1]), out_dtype),
        compiler_params=_params(("arbitrary",)),
        name="sample_matmul",
    )(a, w)


def _page_copies(pt_ref, ckv_hbm, kpe_hbm, ckv_buf, kpe_buf, sems, step, slot, *, n_chunk, pc, page):
    b = step // n_chunk
    c = step % n_chunk
    copies = []
    for p in range(pc):
        pg = pt_ref[b, c * pc + p]
        copies.append(pltpu.make_async_copy(ckv_hbm.at[pg], ckv_buf.at[slot, pl.ds(p * page, page), :], sems.at[slot, 0]))
        copies.append(pltpu.make_async_copy(kpe_hbm.at[pg], kpe_buf.at[slot, :, pl.ds(p * page, page)], sems.at[slot, 1]))
    return copies


def _attn_decode_body(pt_ref, *refs, **kw):
    _run_phases(_decode_phases(pl.program_id(0), pl.num_programs(0), pt_ref, *refs, **kw))


def _decode_phases(s, total, pt_ref, ql_ref, qp_ref, cn_ref, kn_ref, ckv_hbm, kpe_hbm, o_ref,
                   ckv_buf, kpe_buf, sems, m_scr, l_scr, acc_scr, *, layer, n_chunk, pc, page):
    slot = s % 2
    c = s % n_chunk
    copies = functools.partial(_page_copies, pt_ref, ckv_hbm.at[layer], kpe_hbm.at[layer], ckv_buf, kpe_buf, sems,
                               n_chunk=n_chunk, pc=pc, page=page)

    def prologue():
        @pl.when(s == 0)
        def _():
            for cp in copies(s, slot):
                cp.start()

        @pl.when(s + 1 < total)
        def _():
            for cp in copies(s + 1, 1 - slot):
                cp.start()

        for cp in copies(s, slot):
            cp.wait()

        @pl.when(c == 0)
        def _():
            cn = cn_ref[0]
            s_self = (jnp.sum(ql_ref[0] * cn, axis=-1, keepdims=True)
                      + jnp.sum(qp_ref[0] * kn_ref[0], axis=-1, keepdims=True))
            m_scr[...] = s_self
            l_scr[...] = jnp.ones_like(l_scr)
            acc_scr[...] = jnp.broadcast_to(cn, acc_scr.shape)

    n_keys = pc * page
    part = n_keys // DECODE_KEY_PARTS if n_keys % (DECODE_KEY_PARTS * LANES) == 0 else n_keys
    n_parts = n_keys // part
    scores, stats = {}, {}

    def part_scores(i):
        ks = slice(i * part, (i + 1) * part)
        scores[i] = _dot_nt(ql_ref[0], ckv_buf[slot, ks, :]) + _dot(qp_ref[0], kpe_buf[slot, :, ks])

    def part_pv(i):
        ks = slice(i * part, (i + 1) * part)
        m_i = jnp.max(scores[i], axis=-1, keepdims=True)
        p = jnp.exp2(scores[i] - m_i)
        stats[i] = (m_i, jnp.sum(p, axis=-1, keepdims=True), _dot(p, ckv_buf[slot, ks, :]))

    def merge():
        m_old = m_scr[...]
        m_new = m_old
        for i in range(n_parts):
            m_new = jnp.maximum(m_new, stats[i][0])
        alpha = jnp.exp2(m_old - m_new)
        l_new = alpha * l_scr[...]
        acc_new = alpha * acc_scr[...]
        for i in range(n_parts):
            m_i, l_i, acc_i = stats[i]
            w_i = jnp.exp2(m_i - m_new)
            l_new = l_new + w_i * l_i
            acc_new = acc_new + w_i * acc_i
        l_scr[...] = l_new
        acc_scr[...] = acc_new
        m_scr[...] = m_new

    def epilogue():
        @pl.when(c == n_chunk - 1)
        def _():
            o_ref[0] = acc_scr[...] / l_scr[...]

    segments = ([functools.partial(part_scores, i) for i in range(n_parts)]
                + [functools.partial(part_pv, i) for i in range(n_parts)] + [merge])
    return prologue, segments, epilogue


def _decode_pages_per_step(n_pages, page):
    budget = DECODE_BUFFER_BYTES // (2 * page * (A_KV_LORA + A_ROPE) * 4)
    return max(p for p in range(1, n_pages + 1) if n_pages % p == 0 and p <= max(budget, 1))


def _attn_decode(page_table, q_lat, q_pe, ckv_new, kpe_new, pool_ckv, pool_kpe_t, layer):
    db, n_pages = page_table.shape
    page = pool_ckv.shape[2]
    pc = _decode_pages_per_step(n_pages, page)
    n_chunk = n_pages // pc
    per_row = lambda s, pt: (s // n_chunk, 0, 0)
    grid_spec = pltpu.PrefetchScalarGridSpec(
        num_scalar_prefetch=1,
        grid=(db * n_chunk,),
        in_specs=[
            pl.BlockSpec((1, A_HEADS, A_KV_LORA), per_row), pl.BlockSpec((1, A_HEADS, A_ROPE), per_row),
            pl.BlockSpec((1, 1, A_KV_LORA), per_row), pl.BlockSpec((1, 1, A_ROPE), per_row),
            pl.BlockSpec(memory_space=pl.ANY), pl.BlockSpec(memory_space=pl.ANY),
        ],
        out_specs=pl.BlockSpec((1, A_HEADS, A_KV_LORA), per_row),
        scratch_shapes=_decode_scratch(pc, page),
    )
    return pl.pallas_call(
        functools.partial(_attn_decode_body, layer=layer, n_chunk=n_chunk, pc=pc, page=page),
        grid_spec=grid_spec,
        out_shape=jax.ShapeDtypeStruct((db, A_HEADS, A_KV_LORA), F32),
        compiler_params=_params(("arbitrary",)),
        name="mla_paged_decode",
    )(page_table, q_lat, q_pe, ckv_new, kpe_new, pool_ckv, pool_kpe_t)


def _decode_scratch(pc, page):
    return [
        pltpu.VMEM((2, pc * page, A_KV_LORA), F32), pltpu.VMEM((2, A_ROPE, pc * page), F32),
        pltpu.SemaphoreType.DMA((2, 2)),
        pltpu.VMEM((A_HEADS, 1), F32), pltpu.VMEM((A_HEADS, 1), F32), pltpu.VMEM((A_HEADS, A_KV_LORA), F32),
    ]


def _hgrn_decode_body(pt_ref, q_ref, f_ref, i_ref, gt_ref, go_ref, ql_ref, qp_ref, cn_ref, kn_ref, ckv_hbm, kpe_hbm,
                      o_ref, st_ref, ol_ref, st_scr, ckv_buf, kpe_buf, sems, m_scr, l_scr, acc_scr, *, n_sb, decode_kw):
    step = pl.program_id(0) * pl.num_programs(1) + pl.program_id(1)
    total = pl.num_programs(0) * pl.num_programs(1)
    dec_pro, dec_segs, dec_epi = _decode_phases(step, total, pt_ref, ql_ref, qp_ref, cn_ref, kn_ref, ckv_hbm, kpe_hbm,
                                                 ol_ref, ckv_buf, kpe_buf, sems, m_scr, l_scr, acc_scr, **decode_kw)
    hgr_pro, hgr_segs, hgr_epi = _hgrn_phases(q_ref, f_ref, i_ref, gt_ref, go_ref, o_ref, st_ref, st_scr, n_sb=n_sb)
    dec_pro()
    hgr_pro()
    n_prep = len(hgr_segs) // 2
    n_score = (len(dec_segs) - 1) // 2
    order = (dec_segs[:1] + hgr_segs[:n_prep] + dec_segs[1:n_score]
             + _interleave(dec_segs[n_score:], hgr_segs[n_prep:]))
    for seg in order:
        seg()
    dec_epi()
    hgr_epi()


def _interleave(a, b):
    out, j = [], 0
    for i, seg in enumerate(a):
        out.append(seg)
        upto = (i + 1) * len(b) // len(a)
        out.extend(b[j:upto])
        j = upto
    return out


def _merged_chunks(n_steps, db, n_pages, page):
    if n_steps % db:
        return None
    n_chunk = n_steps // db
    if n_pages % n_chunk or n_pages // n_chunk > _decode_pages_per_step(n_pages, page):
        return None
    return n_chunk


def _hgrn_prompt_with_decode(hq, hf, hi, hg, g_out, page_table, q_lat, q_pe, ckv_new, kpe_new, pool_ckv, pool_kpe_t,
                             layer, n_chunk, rows_per_step):
    b, s, hb = hq.shape
    ts = rows_per_step
    nt = s // ts
    db, n_pages = page_table.shape
    page = pool_ckv.shape[2]
    pc = n_pages // n_chunk
    blk = lambda i, t, pt: (i, t, 0)
    per_row = lambda i, t, pt: ((i * nt + t) // n_chunk, 0, 0)
    grid_spec = pltpu.PrefetchScalarGridSpec(
        num_scalar_prefetch=1,
        grid=(b, nt),
        in_specs=[pl.BlockSpec((1, ts, hb), blk)] * 4 + [
            pl.BlockSpec((1, B_DV), lambda i, t, pt: (0, 0)),
            pl.BlockSpec((1, A_HEADS, A_KV_LORA), per_row), pl.BlockSpec((1, A_HEADS, A_ROPE), per_row),
            pl.BlockSpec((1, 1, A_KV_LORA), per_row), pl.BlockSpec((1, 1, A_ROPE), per_row),
            pl.BlockSpec(memory_space=pl.ANY), pl.BlockSpec(memory_space=pl.ANY),
        ],
        out_specs=[
            pl.BlockSpec((1, ts, hb), blk),
            pl.BlockSpec((1, B_HEADS, B_DK, B_DV), lambda i, t, pt: (i, 0, 0, 0)),
            pl.BlockSpec((1, A_HEADS, A_KV_LORA), per_row),
        ],
        scratch_shapes=[pltpu.VMEM((B_HEADS, B_DV, B_DK), F32)] + _decode_scratch(pc, page),
    )
    decode_kw = dict(layer=layer, n_chunk=n_chunk, pc=pc, page=page)
    return pl.pallas_call(
        functools.partial(_hgrn_decode_body, n_sb=ts // HGRN_BLOCK, decode_kw=decode_kw),
        grid_spec=grid_spec,
        out_shape=[
            jax.ShapeDtypeStruct((b, s, hb), BF16),
            jax.ShapeDtypeStruct((b, B_HEADS, B_DK, B_DV), F32),
            jax.ShapeDtypeStruct((db, A_HEADS, A_KV_LORA), F32),
        ],
        compiler_params=_params(("arbitrary", "arbitrary")),
        name="hgrn2_prompt_with_decode",
    )(page_table, hq, hf, hi, hg, g_out.reshape(1, B_DV), q_lat, q_pe, ckv_new, kpe_new, pool_ckv, pool_kpe_t)


def _cmix_prompt_body(x_ref, g_ref, win_ref, wc_ref, wout_ref, o_ref, buf_ref, carry, *, n_tiles):
    t = pl.program_id(1)
    tm = x_ref.shape[1]
    cw = wout_ref.shape[0]

    @pl.when(t == 0)
    def _():
        carry[...] = jnp.zeros_like(carry)

    x = x_ref[0]
    h = _rms(x, g_ref[...]).astype(BF16)
    p = _dot(h, win_ref[...])
    v = p[:, cw:2 * cw] * p[:, 2 * cw:]
    row = lax.broadcasted_iota(jnp.int32, v.shape, 0)
    prev1 = jnp.broadcast_to(carry[1:2, :], v.shape)
    prev2 = jnp.broadcast_to(carry[0:1, :], v.shape)
    v1 = jnp.where(row == 0, prev1, pltpu.roll(v, 1, 0))
    v2 = jnp.where(row == 0, prev2, jnp.where(row == 1, prev1, pltpu.roll(v, 2, 0)))
    y = wc_ref[0:1, :] * v2 + wc_ref[1:2, :] * v1 + wc_ref[2:3, :] * v
    o_ref[0] = x + _dot((p[:, :cw] * y).astype(BF16), wout_ref[...])
    carry[0:2, :] = v[tm - 2:, :]

    @pl.when(t == n_tiles - 1)
    def _():
        buf_ref[0] = v[tm - 2:, :]


def _cmix_prompt(x, g, w_in, w_conv, w_out, tm=1024):
    b, s, d = x.shape
    tm = _row_tile(s, tm)
    n_tiles = s // tm
    cw = w_out.shape[0]
    blk = lambda i, t: (i, t, 0)
    cst = lambda shape: pl.BlockSpec(shape, lambda i, t: (0,) * len(shape), pipeline_mode=pl.Buffered(1))
    return pl.pallas_call(
        functools.partial(_cmix_prompt_body, n_tiles=n_tiles),
        grid=(b, n_tiles),
        in_specs=[pl.BlockSpec((1, tm, d), blk), cst((1, d)), cst(w_in.shape), cst(w_conv.shape), cst(w_out.shape)],
        out_specs=[pl.BlockSpec((1, tm, d), blk), pl.BlockSpec((1, C_KERNEL - 1, cw), lambda i, t: (i, 0, 0))],
        out_shape=[jax.ShapeDtypeStruct((b, s, d), F32), jax.ShapeDtypeStruct((b, C_KERNEL - 1, cw), F32)],
        scratch_shapes=[pltpu.VMEM((8, cw), F32)],
        compiler_params=_params(("parallel", "arbitrary")),
        name="conv_mixer_prompt",
    )(x, g.reshape(1, d), w_in, w_conv, w_out)


def _cmix_step_body(x_ref, b0_ref, b1_ref, g_ref, win_ref, wc_ref, wout_ref, o_ref, v_ref):
    cw = wout_ref.shape[0]
    x = x_ref[...]
    h = _rms(x, g_ref[...]).astype(BF16)
    p = _dot(h, win_ref[...])
    v = p[:, cw:2 * cw] * p[:, 2 * cw:]
    y = wc_ref[0:1, :] * b0_ref[...] + wc_ref[1:2, :] * b1_ref[...] + wc_ref[2:3, :] * v
    o_ref[...] = x + _dot((p[:, :cw] * y).astype(BF16), wout_ref[...])
    v_ref[...] = v


def _cmix_step(x, buf0, buf1, g, w_in, w_conv, w_out):
    n, d = x.shape
    cw = w_out.shape[0]
    args = (x, buf0, buf1, g.reshape(1, d), w_in, w_conv, w_out)
    return pl.pallas_call(
        _cmix_step_body,
        grid=(1,),
        in_specs=[_const_spec(a.shape) for a in args],
        out_specs=[pl.BlockSpec((n, d), lambda i: (0, 0)), pl.BlockSpec((n, cw), lambda i: (0, 0))],
        out_shape=[jax.ShapeDtypeStruct((n, d), F32), jax.ShapeDtypeStruct((n, cw), F32)],
        compiler_params=_params(("arbitrary",)),
        name="conv_mixer_step",
    )(*args)


def _prep_ab_weights(w_in, g_q, g_kv, w_uq, w_ukv, w_out):
    d = w_in.shape[0]
    o_kv = A_Q_LORA
    o_pe = o_kv + A_KV_LORA
    o_h = o_pe + A_ROPE
    zn = jnp.zeros((d, A_NOPE), F32)
    zt = jnp.zeros((d, HEAD_SLOT - A_NOPE - A_ROPE), F32)
    w_in_p = jnp.concatenate([w_in[:, :o_pe], zn, w_in[:, o_pe:o_h], zt, w_in[:, o_h:]], axis=1).astype(BF16)

    wq = w_uq.reshape(A_Q_LORA, A_HEADS, A_NOPE + A_ROPE)
    zq = jnp.zeros((A_Q_LORA, A_HEADS, HEAD_SLOT - A_NOPE - A_ROPE), F32)
    w_q = jnp.concatenate([wq, zq], axis=-1).reshape(A_Q_LORA, A_HEADS * HEAD_SLOT).astype(BF16)

    wkv = w_ukv.reshape(A_KV_LORA, A_HEADS, A_NOPE + A_V)
    w_uk, w_uv = wkv[..., :A_NOPE], wkv[..., A_NOPE:]
    k_pad = jnp.concatenate([w_uk, jnp.zeros((A_KV_LORA, A_HEADS, HEAD_SLOT - A_NOPE), F32)], axis=-1)
    w_kv = jnp.concatenate(
        [k_pad.reshape(A_KV_LORA, A_HEADS * HEAD_SLOT), w_uv.reshape(A_KV_LORA, A_HEADS * A_V)], axis=1).astype(BF16)

    eye = jnp.eye(A_HEADS, dtype=F32)
    uk_t = jnp.transpose(w_uk, (1, 2, 0))
    uk_t = jnp.concatenate([uk_t, jnp.zeros((A_HEADS, HEAD_SLOT - A_NOPE, A_KV_LORA), F32)], axis=1)
    absorb = (uk_t[:, :, None, :] * eye[:, None, :, None]).reshape(A_HEADS * HEAD_SLOT, A_HEADS * A_KV_LORA)
    sel = jnp.zeros((HEAD_SLOT, A_ROPE), F32).at[A_NOPE + jnp.arange(A_ROPE), jnp.arange(A_ROPE)].set(1.0)
    sel = (sel[None, :, None, :] * eye[:, None, :, None]).reshape(A_HEADS * HEAD_SLOT, A_HEADS * A_ROPE)
    w_dec_q = jnp.concatenate([absorb, sel], axis=1).astype(BF16)
    uv_t = jnp.transpose(w_uv, (1, 0, 2))
    w_dec_o = (uv_t[:, :, None, :] * eye[:, None, :, None]).reshape(A_HEADS * A_KV_LORA, A_HEADS * A_V).astype(BF16)

    n_a = A_HEADS * A_V
    return dict(w_in=w_in_p, g_q=g_q.reshape(1, -1), g_kv=g_kv.reshape(1, -1), w_q=w_q, w_kv=w_kv,
                w_dec_q=w_dec_q, w_dec_o=w_dec_o, w_out_a=w_out[:n_a].astype(BF16), w_out_b=w_out[n_a:].astype(BF16))


def _rope_tables(pos):
    inv = ROPE_BASE ** (-jnp.arange(0, A_ROPE, 2, dtype=F32) / A_ROPE)
    ang = pos.astype(F32)[:, None] * inv[None, :]
    cos, sin = jnp.cos(ang), jnp.sin(ang)
    n = pos.shape[0]
    zero = jnp.zeros_like(sin)
    head = jnp.zeros((n, A_NOPE), F32)
    tail = jnp.zeros((n, HEAD_SLOT - A_NOPE - A_ROPE), F32)
    cos_t = jnp.concatenate([jnp.ones((n, A_NOPE), F32), cos, cos, tail], axis=1)
    sin_lo = jnp.concatenate([head, -sin, zero, tail], axis=1)
    sin_hi = jnp.concatenate([head, zero, sin, tail], axis=1)
    c = SM_SCALE * LOG2_E
    return cos_t * c, sin_lo * c, sin_hi * c, cos_t, sin_lo, sin_hi


def kernel(x_prompt, x_sample, cache_ckv, cache_kpe, state_hgrn, state_conv, page_table, norm_g, final_norm_g,
           w_ffn_gate, w_ffn_up, w_ffn_down, w_in_ab, g_q_lora, g_kv_lora, w_uq, w_ukv, hgrn_lower_bound,
           g_hgrn_out, w_out_ab, w_in_c, w_conv_c, w_out_c):
    b_p, s_p, d = x_prompt.shape
    d_b, s_d, _ = x_sample.shape
    assert s_d == 1
    depth = norm_g.shape[0]
    past_len = page_table.shape[1] * cache_ckv.shape[2]
    n_p = b_p * s_p

    tab_p = _rope_tables(jnp.arange(s_p))
    tab_d = _rope_tables(jnp.full((d_b,), past_len))
    pool_kpe_t = jnp.swapaxes(cache_kpe, 2, 3)

    w_ffn = (w_ffn_gate.astype(BF16), w_ffn_up.astype(BF16), w_ffn_down.astype(BF16))

    xp = x_prompt.reshape(n_p, d)
    xd = x_sample.reshape(d_b, d)
    outs = {k: [] for k in ("ckv_p", "kpe_p", "ckv_d", "kpe_d", "hg_p", "hg_d", "cv_p", "cv_d")}

    for layer in range(depth):
        j = layer // 2
        last = layer == depth - 1
        xp = _ffn(xp, norm_g[layer, 0], w_ffn, (layer, 0))
        xd = _ffn(xd, norm_g[layer, 0], w_ffn, (layer, 0))
        ffn_b = (norm_g[layer, 2], w_ffn, (layer, 1))
        fin = final_norm_g if last else None
        if layer % 2 == 0:
            w = _prep_ab_weights(w_in_ab[j], g_q_lora[j], g_kv_lora[j], w_uq[j], w_ukv[j], w_out_ab[j])
            ckv, kpe_t, q, k, v, hq, hf, hi, hg = _ab_pre(xp, norm_g[layer, 1], w, tab_p, hgrn_lower_bound, layer)
            outs["ckv_p"].append(ckv.reshape(b_p, s_p, -1))
            outs["kpe_p"].append(jnp.swapaxes(kpe_t, 1, 2))
            o_a = _attn_prompt(q.reshape(b_p, s_p, -1), k.reshape(b_p, s_p, -1), v.reshape(b_p, s_p, -1))
            sh = (b_p, s_p, -1)
            hgrn_in = (hq.reshape(sh), hf.reshape(sh), hi.reshape(sh), hg.reshape(sh), g_hgrn_out[j])
            ckv, kpe_t, q, _, _, hq, hf, hi, hg = _ab_pre(xd, norm_g[layer, 1], w, tab_d, hgrn_lower_bound, layer)
            kpe = kpe_t[0].T
            qd = _small_matmul(q, w["w_dec_q"], F32)
            n_lat = A_HEADS * A_KV_LORA
            dec_in = (page_table, qd[:, :n_lat].reshape(d_b, A_HEADS, A_KV_LORA), qd[:, n_lat:].reshape(d_b, A_HEADS, A_ROPE),
                      ckv.reshape(d_b, 1, -1), kpe.reshape(d_b, 1, -1), cache_ckv, pool_kpe_t, j)
            ts = _row_tile(s_p, HGRN_ROWS_PER_STEP)
            n_chunk = _merged_chunks(b_p * (s_p // ts), d_b, page_table.shape[1], cache_ckv.shape[2])
            if n_chunk is None:
                o_b_p, st_p = _hgrn_prompt(*hgrn_in, rows_per_step=ts)
                o_lat = _attn_decode(*dec_in)
            else:
                o_b_p, st_p, o_lat = _hgrn_prompt_with_decode(*hgrn_in, *dec_in, n_chunk, ts)
            outs["hg_p"].append(st_p)
            pre = ((o_a.reshape(n_p, -1), w["w_out_a"]), (o_b_p.reshape(n_p, -1), w["w_out_b"]))
            xp = _ffn(xp, *ffn_b, pre=pre, final_g=fin)
            o_a = _small_matmul(o_lat.reshape(d_b, n_lat), w["w_dec_o"], BF16)
            o_b, st_d = _hgrn_step(hq, hf, hi, hg, g_hgrn_out[j], state_hgrn, j)
            xd = _ffn(xd, *ffn_b, pre=((o_a, w["w_out_a"]), (o_b, w["w_out_b"])), final_g=fin)
            outs["ckv_d"].append(ckv)
            outs["kpe_d"].append(kpe)
            outs["hg_d"].append(st_d)
        else:
            win = w_in_c[j].astype(BF16)
            wout = w_out_c[j].astype(BF16)
            xp3, buf_p = _cmix_prompt(xp.reshape(b_p, s_p, d), norm_g[layer, 1], win, w_conv_c[j], wout)
            xp = _ffn(xp3.reshape(n_p, d), *ffn_b, final_g=fin)
            xd, v_new = _cmix_step(xd, state_conv[j, :, 0], state_conv[j, :, 1], norm_g[layer, 1], win, w_conv_c[j], wout)
            xd = _ffn(xd, *ffn_b, final_g=fin)
            outs["cv_p"].append(buf_p)
            outs["cv_d"].append(jnp.stack([state_conv[j, :, 1], v_new], axis=1))

    return (xp.reshape(b_p, s_p, d), xd.reshape(d_b, s_d, d),
            jnp.stack(outs["ckv_p"]), jnp.stack(outs["kpe_p"]),
            jnp.stack(outs["ckv_d"]).reshape(-1, d_b, s_d, A_KV_LORA), jnp.stack(outs["kpe_d"]).reshape(-1, d_b, s_d, A_ROPE),
            jnp.stack(outs["hg_p"]), jnp.stack(outs["hg_d"]), jnp.stack(outs["cv_p"]), jnp.stack(outs["cv_d"]))
```
